```python
import math
import jax
import jax.numpy as jnp
from jax import lax
import numpy as np

D_MODEL = 4096
BATCH = 4
SEQ = 2048
DEPTH = 2
DEC_BATCH = 128
DEC_SEQ = 1
PAST_LEN = 16384
PAGE_SIZE = 128

HEAD_DIM = 64
N_HEADS_A = 8
N_HEADS_B = 8
N_HEADS_C = 8
N_HEADS_D = 8
N_IDX_HEADS = 16
IDX_DIM = 32
DSA_TOPK = 256
MOBA_BLOCK = 256
MOBA_TOPK = 3
MLA_Q_RANK = 384
MLA_KV_RANK = 128
MLA_NOPE = 64
MLA_ROPE = 32
MLA_V = 64
ROPE_THETA = 10000.0
N_BUCKETS = 32
MAX_DISTANCE = 128
N_BIAS_HEADS = N_HEADS_A + N_HEADS_C
N_BRANCH = 4
BRANCH_WIDTH = N_HEADS_A * HEAD_DIM
PEER_HEADS = 8
PEER_KEY_DIM = 256
PEER_N_KEYS = 128
PEER_N_EXPERTS = PEER_N_KEYS * PEER_N_KEYS
PEER_TOPK = 16
Q_BLOCK = 128
MOBA_Q_BLOCK = 16
PEER_BLOCK = 64
FORGET_BIAS = 3.0
EPS = 1e-6
NEG = -1e30

kernel_name = 'hybrid_gated_dsa_fox_moba_mla_peer_step'

IN_LAYOUT = (
    ('q_a', N_HEADS_A * HEAD_DIM), ('k_a', HEAD_DIM), ('v_a', HEAD_DIM),
    ('q_idx', N_IDX_HEADS * IDX_DIM), ('k_idx', IDX_DIM), ('w_idx', N_IDX_HEADS),
    ('q_b', N_HEADS_B * HEAD_DIM), ('k_b', HEAD_DIM), ('v_b', HEAD_DIM), ('f_b', N_HEADS_B),
    ('q_c', N_HEADS_C * HEAD_DIM), ('k_c', HEAD_DIM), ('v_c', HEAD_DIM),
    ('cq_d', MLA_Q_RANK), ('ckv_d', MLA_KV_RANK), ('kr_d', MLA_ROPE),
    ('gates', N_BRANCH * D_MODEL),
)
D_IN = sum(size for _, size in IN_LAYOUT)


def split_columns(z):
    parts = {}
    off = 0
    for name, size in IN_LAYOUT:
        parts[name] = z[..., off:off + size]
        off += size
    return parts


def rmsnorm(x, g):
    xf = x.astype(jnp.float32)
    y = xf * lax.rsqrt(jnp.mean(xf * xf, axis=-1, keepdims=True) + EPS)
    return (y * g.astype(jnp.float32)).astype(x.dtype)


def rel_bucket(rel):
    n_exact = N_BUCKETS // 2
    relf = jnp.maximum(rel, 1).astype(jnp.float32)
    large = n_exact + (jnp.log(relf / n_exact) / math.log(MAX_DISTANCE / n_exact)
                       * (N_BUCKETS - n_exact)).astype(jnp.int32)
    large = jnp.minimum(large, N_BUCKETS - 1)
    return jnp.where(rel < n_exact, jnp.maximum(rel, 0), large)


def apply_rope(x, pos):
    half = x.shape[-1] // 2
    inv_freq = ROPE_THETA ** (-jnp.arange(half, dtype=jnp.float32) / half)
    ang = pos[..., None].astype(jnp.float32) * inv_freq
    cos = jnp.cos(ang)[:, :, None, :]
    sin = jnp.sin(ang)[:, :, None, :]
    x1 = x[..., :half].astype(jnp.float32)
    x2 = x[..., half:].astype(jnp.float32)
    return jnp.concatenate([x1 * cos - x2 * sin, x2 * cos + x1 * sin], axis=-1).astype(x.dtype)


def map_query_blocks(fn, arrays, blk):
    n_q = arrays[0].shape[1]
    blk = min(blk, n_q)
    n_blk = -(-n_q // blk)
    pad = n_blk * blk - n_q

    def to_blocks(a):
        a = jnp.pad(a, [(0, 0), (0, pad)] + [(0, 0)] * (a.ndim - 2))
        a = a.reshape((a.shape[0], n_blk, blk) + a.shape[2:])
        return jnp.moveaxis(a, 1, 0)

    out = lax.map(lambda xs: fn(*xs), tuple(to_blocks(a) for a in arrays))
    out = jnp.moveaxis(out, 0, 1)
    out = out.reshape((out.shape[0], n_blk * blk) + out.shape[3:])
    return out[:, :n_q]


def gather_rows(rows, pos):
    return jax.vmap(lambda r, p: r[p])(rows, pos)


def paged_context(cache, l, page_table, new_rows):
    past = cache[l, page_table]
    past = past.reshape((past.shape[0], past.shape[1] * past.shape[2]) + past.shape[3:])
    return jnp.concatenate([past.astype(new_rows.dtype), new_rows], axis=1)


def paged_fetch(cache, l, page_table, new_rows, pos):
    past_len = page_table.shape[1] * PAGE_SIZE
    pp = jnp.minimum(pos, past_len - 1)
    flat = pp.reshape(pp.shape[0], -1)
    page = jnp.take_along_axis(page_table, flat // PAGE_SIZE, axis=1).reshape(pos.shape)
    old = cache[l, page, pp % PAGE_SIZE].astype(new_rows.dtype)
    new = gather_rows(new_rows, jnp.clip(pos - past_len, 0, new_rows.shape[1] - 1))
    keep = (pos < past_len).reshape(pos.shape + (1,) * (old.ndim - pos.ndim))
    return jnp.where(keep, old, new)


def dsa_attention(q, q_idx, w_idx, qpos, k_idx_ctx, fetch_kv, t5_tab):
    n_keys = k_idx_ctx.shape[1]
    k_sel = min(DSA_TOPK, n_keys // 4)
    kpos = jnp.arange(n_keys, dtype=jnp.int32)
    kidx = k_idx_ctx.astype(jnp.float32)
    scale = HEAD_DIM ** -0.5

    def block(qb, qib, wb, pb):
        dots = jnp.einsum('bthd,bsd->bths', qib.astype(jnp.float32), kidx)
        score = jnp.einsum('bth,bths->bts', wb.astype(jnp.float32), jax.nn.relu(dots))
        score = jnp.where(kpos <= pb[..., None], score, NEG)
        _, sel = lax.top_k(score, k_sel)
        kv = fetch_kv(sel)
        rel = pb[..., None] - sel
        s = jnp.einsum('bthd,btkd->bthk', qb, kv[..., 0, :]).astype(jnp.float32) * scale
        s = s + jnp.moveaxis(t5_tab[rel_bucket(rel)], -1, 2).astype(jnp.float32)
        s = jnp.where((rel >= 0)[:, :, None, :], s, NEG)
        p = jax.nn.softmax(s, axis=-1).astype(kv.dtype)
        return jnp.einsum('bthk,btkd->bthd', p, kv[..., 1, :])

    return map_query_blocks(block, (q, q_idx, w_idx, qpos), Q_BLOCK)


def forgetting_attention(q, f_q, qpos, kv_ctx, f_ctx):
    n_keys = kv_ctx.shape[1]
    kpos = jnp.arange(n_keys, dtype=jnp.int32)
    k = kv_ctx[..., 0, :]
    v = kv_ctx[..., 1, :]
    f_k = jnp.moveaxis(f_ctx, 1, 2)
    scale = HEAD_DIM ** -0.5

    def block(qb, fb, pb):
        s = jnp.einsum('bthd,bsd->bhts', qb, k).astype(jnp.float32) * scale
        s = s + jnp.moveaxis(fb, 1, 2)[..., None] - f_k[:, :, None, :]
        s = jnp.where(kpos <= pb[:, None, :, None], s, NEG)
        p = jax.nn.softmax(s, axis=-1).astype(v.dtype)
        return jnp.einsum('bhts,bsd->bthd', p, v)

    return map_query_blocks(block, (q, f_q, qpos), Q_BLOCK)


def moba_attention(q, qpos, kv_ctx, t5_tab):
    B, n_keys = kv_ctx.shape[0], kv_ctx.shape[1]
    n_blk = -(-n_keys // MOBA_BLOCK)
    k_sel = min(MOBA_TOPK, n_blk)
    k_pad = jnp.pad(kv_ctx[..., 0, :].astype(jnp.float32),
                    ((0, 0), (0, n_blk * MOBA_BLOCK - n_keys), (0, 0)))
    k_mean = k_pad.reshape(B, n_blk, MOBA_BLOCK, HEAD_DIM).mean(axis=2)
    blk_ids = jnp.arange(n_blk, dtype=jnp.int32)
    offs = jnp.arange(MOBA_BLOCK, dtype=jnp.int32)
    t5_heads = t5_tab.T
    head_ids = jnp.arange(N_HEADS_C)[:, None]
    scale = HEAD_DIM ** -0.5

    def block(qb, pb):
        Bq, Tb = pb.shape
        cur = pb // MOBA_BLOCK
        gate = jnp.einsum('bthd,bnd->bthn', qb.astype(jnp.float32), k_mean)
        gate = jnp.where(blk_ids < cur[..., None, None], gate, NEG)
        _, sel = lax.top_k(gate, k_sel)
        sel_ok = jnp.arange(k_sel) < cur[..., None, None]
        pos_sel = (sel[..., None] * MOBA_BLOCK + offs).reshape(Bq, Tb, N_HEADS_C, k_sel * MOBA_BLOCK)
        ok_sel = jnp.broadcast_to(jnp.repeat(sel_ok, MOBA_BLOCK, axis=-1), pos_sel.shape)
        pos_own = jnp.broadcast_to((cur * MOBA_BLOCK)[..., None, None] + offs,
                                   (Bq, Tb, N_HEADS_C, MOBA_BLOCK))
        ok_own = pos_own <= pb[..., None, None]
        pos = jnp.concatenate([pos_sel, pos_own], axis=-1)
        ok = jnp.concatenate([ok_sel, ok_own], axis=-1)
        kv = gather_rows(kv_ctx, jnp.clip(pos, 0, n_keys - 1))
        s = jnp.einsum('bthd,bthkd->bthk', qb, kv[..., 0, :]).astype(jnp.float32) * scale
        rel = pb[..., None, None] - pos
        s = s + t5_heads[head_ids, rel_bucket(rel)].astype(jnp.float32)
        s = jnp.where(ok, s, NEG)
        p = jax.nn.softmax(s, axis=-1).astype(kv.dtype)
        return jnp.einsum('bthk,bthkd->bthd', p, kv[..., 1, :])

    return map_query_blocks(block, (q, qpos), MOBA_Q_BLOCK)


def latent_attention(q_lat, q_rope, qpos, lat_ctx, rope_ctx):
    n_keys = lat_ctx.shape[1]
    kpos = jnp.arange(n_keys, dtype=jnp.int32)
    scale = (MLA_NOPE + MLA_ROPE) ** -0.5

    def block(qlb, qrb, pb):
        s = (jnp.einsum('bthc,bsc->bhts', qlb, lat_ctx)
             + jnp.einsum('bthr,bsr->bhts', qrb, rope_ctx)).astype(jnp.float32) * scale
        s = jnp.where(kpos <= pb[:, None, :, None], s, NEG)
        p = jax.nn.softmax(s, axis=-1).astype(lat_ctx.dtype)
        return jnp.einsum('bhts,bsc->bthc', p, lat_ctx)

    return map_query_blocks(block, (q_lat, q_rope, qpos), Q_BLOCK)


def token_mixers(h, l, W, past):
    B, T, _ = h.shape
    f32 = jnp.float32
    z = split_columns(jnp.einsum('btd,de->bte', h, W['w_in'][l]))
    n_past = 0 if past is None else past[1].shape[1] * PAGE_SIZE
    qpos = jnp.broadcast_to(n_past + jnp.arange(T, dtype=jnp.int32), (B, T))
    q_a = z['q_a'].reshape(B, T, N_HEADS_A, HEAD_DIM)
    q_idx = z['q_idx'].reshape(B, T, N_IDX_HEADS, IDX_DIM)
    w_idx = z['w_idx'] * N_IDX_HEADS ** -0.5
    a_kv = jnp.stack([z['k_a'], z['v_a']], axis=2)
    a_idx = z['k_idx']
    q_b = z['q_b'].reshape(B, T, N_HEADS_B, HEAD_DIM)
    b_kv = jnp.stack([z['k_b'], z['v_b']], axis=2)
    b_logf = jax.nn.log_sigmoid(z['f_b'].astype(f32) + W['b_forget'][l].astype(f32)).astype(h.dtype)
    q_c = z['q_c'].reshape(B, T, N_HEADS_C, HEAD_DIM)
    c_kv = jnp.stack([z['k_c'], z['v_c']], axis=2)
    cq = rmsnorm(z['cq_d'], W['g_cq'][l])
    q_d = jnp.einsum('btr,re->bte', cq, W['w_uq'][l]).reshape(B, T, N_HEADS_D, MLA_NOPE + MLA_ROPE)
    q_rope = apply_rope(q_d[..., MLA_NOPE:], qpos)
    w_uk = W['w_ukv'][l][..., :MLA_NOPE]
    w_uv = W['w_ukv'][l][..., MLA_NOPE:]
    q_lat = jnp.einsum('bthn,chn->bthc', q_d[..., :MLA_NOPE], w_uk)
    d_lat = rmsnorm(z['ckv_d'], W['g_ckv'][l])
    d_rope = apply_rope(z['kr_d'][:, :, None, :], qpos)[:, :, 0, :]

    if past is None:
        def ctx(name, rows):
            return rows

        def fetch_a(pos):
            return gather_rows(a_kv, pos)
    else:
        caches, page_table = past

        def ctx(name, rows):
            return paged_context(caches[name], l, page_table, rows)

        def fetch_a(pos):
            return paged_fetch(caches['a_kv'], l, page_table, a_kv, pos)

    t5_a = W['t5_table'][:, :N_HEADS_A]
    t5_c = W['t5_table'][:, N_HEADS_A:]
    o_a = dsa_attention(q_a, q_idx, w_idx, qpos, ctx('a_idx', a_idx), fetch_a, t5_a)
    f_ctx = jnp.cumsum(ctx('b_logf', b_logf).astype(f32), axis=1)
    o_b = forgetting_attention(q_b, f_ctx[:, -T:], qpos, ctx('b_kv', b_kv), f_ctx)
    o_c = moba_attention(q_c, qpos, ctx('c_kv', c_kv), t5_c)
    o_lat = latent_attention(q_lat, q_rope, qpos, ctx('d_lat', d_lat), ctx('d_rope', d_rope))
    o_d = jnp.einsum('bthc,chn->bthn', o_lat, w_uv)
    o = jnp.stack([o_a.reshape(B, T, BRANCH_WIDTH), o_b.reshape(B, T, BRANCH_WIDTH),
                   o_c.reshape(B, T, BRANCH_WIDTH), o_d.reshape(B, T, BRANCH_WIDTH)], axis=2)
    y = jnp.einsum('btnw,nwd->btnd', o, W['w_branch'][l])
    gates = jax.nn.sigmoid(z['gates'].reshape(B, T, N_BRANCH, D_MODEL))
    out = jnp.einsum('btd,de->bte', jnp.sum(gates * y, axis=2), W['w_out'][l])
    return out, (a_kv, a_idx, b_kv, b_logf, c_kv, d_lat, d_rope)


def peer_ffn(h, l, W):
    B, T, D = h.shape
    n = B * T
    hf = h.reshape(1, n, D)
    q = jnp.einsum('znd,de->zne', hf, W['w_peer_q'][l]).reshape(n, PEER_HEADS, PEER_KEY_DIM)
    q = q.astype(jnp.float32)
    half = PEER_KEY_DIM // 2
    s1 = jnp.einsum('nhd,kd->nhk', q[..., :half], W['peer_k1'][l].astype(jnp.float32))
    s2 = jnp.einsum('nhd,kd->nhk', q[..., half:], W['peer_k2'][l].astype(jnp.float32))
    v1, i1 = lax.top_k(s1, PEER_TOPK)
    v2, i2 = lax.top_k(s2, PEER_TOPK)
    cand = (v1[..., :, None] + v2[..., None, :]).reshape(n, PEER_HEADS, PEER_TOPK * PEER_TOPK)
    top, ci = lax.top_k(cand, PEER_TOPK)
    expert = (jnp.take_along_axis(i1, ci // PEER_TOPK, axis=-1) * PEER_N_KEYS
              + jnp.take_along_axis(i2, ci % PEER_TOPK, axis=-1))
    g = jax.nn.softmax(top, axis=-1)
    expert = expert.reshape(1, n, PEER_HEADS * PEER_TOPK)
    g = g.reshape(1, n, PEER_HEADS * PEER_TOPK).astype(h.dtype)
    u_tab = W['peer_u']
    v_tab = W['peer_v']

    def block(hb, eb, gb):
        u = u_tab[l, eb]
        a = jax.nn.gelu(jnp.einsum('znd,zned->zne', hb, u), approximate=False)
        v = v_tab[l, eb]
        return jnp.einsum('zne,zned->znd', gb * a, v)

    out = map_query_blocks(block, (hf, expert, g), PEER_BLOCK)
    return out.reshape(B, T, D)


def run_trunk(x, c, past, W):
    per_layer = []
    for l in range(DEPTH):
        mod = jnp.einsum('bd,de->be', jax.nn.silu(c), W['w_ada'][l]) + W['b_ada'][l]
        sh1, sc1, g1, sh2, sc2, g2 = jnp.split(mod[:, None, :], 6, axis=-1)
        h = rmsnorm(x, W['norm_mix'][l]) * (1.0 + sc1) + sh1
        mix, rows = token_mixers(h, l, W, past)
        x = x + g1 * mix
        h = rmsnorm(x, W['norm_ffn'][l]) * (1.0 + sc2) + sh2
        x = x + g2 * peer_ffn(h, l, W)
        per_layer.append(rows)
    y = rmsnorm(x, W['norm_final'])
    stacked = tuple(jnp.stack([rows[i] for rows in per_layer], axis=0) for i in range(7))
    return y, stacked


def setup_inputs(seed: int = 0) -> dict:
    key = jax.random.key(seed)
    ks = jax.random.split(key, 32)
    f32 = jnp.float32

    def nrm(k, shape, scale=1.0):
        return jax.random.normal(k, shape, f32) * scale

    n_pages = PAST_LEN // PAGE_SIZE
    n_used = DEC_BATCH * n_pages
    n_pool = n_used + max(1, n_used // 4)
    pool = (DEPTH, n_pool, PAGE_SIZE)
    page_table = jax.random.permutation(ks[11], n_pool)[:n_used].reshape(DEC_BATCH, n_pages).astype(jnp.int32)
    return {
        'x_prompt': nrm(ks[0], (BATCH, SEQ, D_MODEL)),
        'x_sample': nrm(ks[1], (DEC_BATCH, DEC_SEQ, D_MODEL)),
        'cache_a_kv': nrm(ks[2], pool + (2, HEAD_DIM)),
        'cache_a_idx': nrm(ks[3], pool + (IDX_DIM,)),
        'cache_b_kv': nrm(ks[4], pool + (2, HEAD_DIM)),
        'cache_b_logf': jax.nn.log_sigmoid(FORGET_BIAS + nrm(ks[5], pool + (N_HEADS_B,))),
        'cache_c_kv': nrm(ks[6], pool + (2, HEAD_DIM)),
        'cache_d_latent': nrm(ks[7], pool + (MLA_KV_RANK,)),
        'cache_d_rope': nrm(ks[8], pool + (MLA_ROPE,)),
        'page_table': page_table,
        'c_prompt': nrm(ks[9], (BATCH, D_MODEL)),
        'c_sample': nrm(ks[10], (DEC_BATCH, D_MODEL)),
        't5_table': nrm(ks[12], (N_BUCKETS, N_BIAS_HEADS), 0.5),
        'w_ada': nrm(ks[13], (DEPTH, D_MODEL, 6 * D_MODEL), 0.3 * D_MODEL ** -0.5),
        'b_ada': nrm(ks[14], (DEPTH, 6 * D_MODEL), 0.02),
        'norm_mix': 1.0 + nrm(ks[15], (DEPTH, D_MODEL), 0.02),
        'norm_ffn': 1.0 + nrm(ks[16], (DEPTH, D_MODEL), 0.02),
        'w_in': nrm(ks[17], (DEPTH, D_MODEL, D_IN), D_MODEL ** -0.5),
        'b_forget': FORGET_BIAS + nrm(ks[18], (DEPTH, N_HEADS_B), 0.1),
        'g_cq': 1.0 + nrm(ks[19], (DEPTH, MLA_Q_RANK), 0.02),
        'g_ckv': 1.0 + nrm(ks[20], (DEPTH, MLA_KV_RANK), 0.02),
        'w_uq': nrm(ks[21], (DEPTH, MLA_Q_RANK, N_HEADS_D * (MLA_NOPE + MLA_ROPE)), MLA_Q_RANK ** -0.5),
        'w_ukv': nrm(ks[22], (DEPTH, MLA_KV_RANK, N_HEADS_D, MLA_NOPE + MLA_V), MLA_KV_RANK ** -0.5),
        'w_branch': nrm(ks[23], (DEPTH, N_BRANCH, BRANCH_WIDTH, D_MODEL), BRANCH_WIDTH ** -0.5),
        'w_out': nrm(ks[24], (DEPTH, D_MODEL, D_MODEL), D_MODEL ** -0.5),
        'w_peer_q': nrm(ks[25], (DEPTH, D_MODEL, PEER_HEADS * PEER_KEY_DIM), D_MODEL ** -0.5),
        'peer_k1': nrm(ks[26], (DEPTH, PEER_N_KEYS, PEER_KEY_DIM // 2), (PEER_KEY_DIM // 2) ** -0.5),
        'peer_k2': nrm(ks[27], (DEPTH, PEER_N_KEYS, PEER_KEY_DIM // 2), (PEER_KEY_DIM // 2) ** -0.5),
        'peer_u': nrm(ks[28], (DEPTH, PEER_N_EXPERTS, D_MODEL), D_MODEL ** -0.5),
        'peer_v': nrm(ks[29], (DEPTH, PEER_N_EXPERTS, D_MODEL), 0.35),
        'norm_final': 1.0 + nrm(ks[30], (D_MODEL,), 0.02),
    }


def reference(x_prompt, x_sample, cache_a_kv, cache_a_idx, cache_b_kv, cache_b_logf, cache_c_kv,
              cache_d_latent, cache_d_rope, page_table, c_prompt, c_sample, t5_table, w_ada, b_ada,
              norm_mix, norm_ffn, w_in, b_forget, g_cq, g_ckv, w_uq, w_ukv, w_branch, w_out,
              w_peer_q, peer_k1, peer_k2, peer_u, peer_v, norm_final):
    W = {
        't5_table': t5_table, 'w_ada': w_ada, 'b_ada': b_ada, 'norm_mix': norm_mix,
        'norm_ffn': norm_ffn, 'w_in': w_in, 'b_forget': b_forget, 'g_cq': g_cq, 'g_ckv': g_ckv,
        'w_uq': w_uq, 'w_ukv': w_ukv, 'w_branch': w_branch, 'w_out': w_out,
        'w_peer_q': w_peer_q, 'peer_k1': peer_k1, 'peer_k2': peer_k2, 'peer_u': peer_u,
        'peer_v': peer_v, 'norm_final': norm_final,
    }
    caches = {
        'a_kv': cache_a_kv, 'a_idx': cache_a_idx, 'b_kv': cache_b_kv, 'b_logf': cache_b_logf,
        'c_kv': cache_c_kv, 'd_lat': cache_d_latent, 'd_rope': cache_d_rope,
    }
    y_prompt, rows_p = run_trunk(x_prompt, c_prompt, None, W)
    y_sample, rows_s = run_trunk(x_sample, c_sample, (caches, page_table), W)
    new_a_kv_p, new_a_idx_p, new_b_kv_p, new_b_logf_p, new_c_kv_p, new_d_lat_p, new_d_rope_p = rows_p
    new_a_kv_s, new_a_idx_s, new_b_kv_s, new_b_logf_s, new_c_kv_s, new_d_lat_s, new_d_rope_s = rows_s
    return (y_prompt, y_sample,
            new_a_kv_p, new_a_idx_p, new_b_kv_p, new_b_logf_p, new_c_kv_p, new_d_lat_p, new_d_rope_p,
            new_a_kv_s, new_a_idx_s, new_b_kv_s, new_b_logf_s, new_c_kv_s, new_d_lat_s, new_d_rope_s)
```

```python
import math
from functools import partial

import jax
import jax.numpy as jnp
from jax import lax
from jax.experimental import pallas as pl
from jax.experimental.pallas import tpu as pltpu

PAGE_SIZE = 128
HEAD_DIM = 64
N_HEADS = 8
N_IDX_HEADS = 16
IDX_DIM = 32
DSA_TOPK = 256
MOBA_BLOCK = 256
MOBA_TOPK = 3
MLA_Q_RANK = 384
MLA_KV_RANK = 128
MLA_NOPE = 64
MLA_ROPE = 32
ROPE_THETA = 10000.0
N_BUCKETS = 32
MAX_DISTANCE = 128
N_BRANCH = 4
BRANCH_WIDTH = N_HEADS * HEAD_DIM
PEER_HEADS = 8
PEER_KEY_DIM = 256
PEER_N_KEYS = 128
PEER_TOPK = 16
Q_BLOCK = 128
MOBA_Q_BLOCK = 16
PEER_BLOCK = 64
EPS = 1e-6
NEG = -1e30

VMEM_LIMIT = 56 * 1024 * 1024


def _in_layout(d_model):
    return (
        ('q_a', N_HEADS * HEAD_DIM), ('k_a', HEAD_DIM), ('v_a', HEAD_DIM),
        ('q_idx', N_IDX_HEADS * IDX_DIM), ('k_idx', IDX_DIM), ('w_idx', N_IDX_HEADS),
        ('q_b', N_HEADS * HEAD_DIM), ('k_b', HEAD_DIM), ('v_b', HEAD_DIM), ('f_b', N_HEADS),
        ('q_c', N_HEADS * HEAD_DIM), ('k_c', HEAD_DIM), ('v_c', HEAD_DIM),
        ('cq_d', MLA_Q_RANK), ('ckv_d', MLA_KV_RANK), ('kr_d', MLA_ROPE),
        ('gates', N_BRANCH * d_model),
    )


def _mm_kernel(a_ref, w_ref, o_ref):
    a = a_ref[...].astype(jnp.bfloat16)
    w = w_ref[...].astype(jnp.bfloat16)
    o_ref[...] = jnp.dot(a, w, preferred_element_type=jnp.float32)


def _round_up(x, m):
    return -(-x // m) * m


def _mm(a, w, tm=512, tn=512):
    M, K = a.shape
    N = w.shape[1]
    tm = min(tm, _round_up(M, 8))
    tn = min(tn, _round_up(N, 128))
    Mp, Np = _round_up(M, tm), _round_up(N, tn)
    if Mp != M:
        a = jnp.pad(a, ((0, Mp - M), (0, 0)))
    if Np != N:
        w = jnp.pad(w, ((0, 0), (0, Np - N)))
    out = pl.pallas_call(
        _mm_kernel,
        grid=(Mp // tm, Np // tn),
        in_specs=[pl.BlockSpec((tm, K), lambda i, j: (i, 0)),
                  pl.BlockSpec((K, tn), lambda i, j: (0, j))],
        out_specs=pl.BlockSpec((tm, tn), lambda i, j: (i, j)),
        out_shape=jax.ShapeDtypeStruct((Mp, Np), jnp.float32),
        compiler_params=pltpu.CompilerParams(
            dimension_semantics=("parallel", "parallel"), vmem_limit_bytes=VMEM_LIMIT),
    )(a, w)
    return out[:M, :N]


def split_columns(z, d_model):
    parts = {}
    off = 0
    for name, size in _in_layout(d_model):
        parts[name] = z[..., off:off + size]
        off += size
    return parts


def rmsnorm(x, g):
    xf = x.astype(jnp.float32)
    y = xf * lax.rsqrt(jnp.mean(xf * xf, axis=-1, keepdims=True) + EPS)
    return (y * g.astype(jnp.float32)).astype(x.dtype)


def rel_bucket(rel):
    n_exact = N_BUCKETS // 2
    relf = jnp.maximum(rel, 1).astype(jnp.float32)
    large = n_exact + (jnp.log(relf / n_exact) / math.log(MAX_DISTANCE / n_exact)
                       * (N_BUCKETS - n_exact)).astype(jnp.int32)
    large = jnp.minimum(large, N_BUCKETS - 1)
    return jnp.where(rel < n_exact, jnp.maximum(rel, 0), large)


def apply_rope(x, pos):
    half = x.shape[-1] // 2
    inv_freq = ROPE_THETA ** (-jnp.arange(half, dtype=jnp.float32) / half)
    ang = pos[..., None].astype(jnp.float32) * inv_freq
    cos = jnp.cos(ang)[:, :, None, :]
    sin = jnp.sin(ang)[:, :, None, :]
    x1 = x[..., :half].astype(jnp.float32)
    x2 = x[..., half:].astype(jnp.float32)
    return jnp.concatenate([x1 * cos - x2 * sin, x2 * cos + x1 * sin], axis=-1).astype(x.dtype)


def map_query_blocks(fn, arrays, blk):
    n_q = arrays[0].shape[1]
    blk = min(blk, n_q)
    n_blk = -(-n_q // blk)
    pad = n_blk * blk - n_q

    def to_blocks(a):
        a = jnp.pad(a, [(0, 0), (0, pad)] + [(0, 0)] * (a.ndim - 2))
        a = a.reshape((a.shape[0], n_blk, blk) + a.shape[2:])
        return jnp.moveaxis(a, 1, 0)

    out = lax.map(lambda xs: fn(*xs), tuple(to_blocks(a) for a in arrays))
    out = jnp.moveaxis(out, 0, 1)
    out = out.reshape((out.shape[0], n_blk * blk) + out.shape[3:])
    return out[:, :n_q]


def gather_rows(rows, pos):
    return jax.vmap(lambda r, p: r[p])(rows, pos)


def paged_context(cache, l, page_table, new_rows):
    past = cache[l, page_table]
    past = past.reshape((past.shape[0], past.shape[1] * past.shape[2]) + past.shape[3:])
    return jnp.concatenate([past.astype(new_rows.dtype), new_rows], axis=1)


def paged_fetch(cache, l, page_table, new_rows, pos):
    past_len = page_table.shape[1] * PAGE_SIZE
    pp = jnp.minimum(pos, past_len - 1)
    flat = pp.reshape(pp.shape[0], -1)
    page = jnp.take_along_axis(page_table, flat // PAGE_SIZE, axis=1).reshape(pos.shape)
    old = cache[l, page, pp % PAGE_SIZE].astype(new_rows.dtype)
    new = gather_rows(new_rows, jnp.clip(pos - past_len, 0, new_rows.shape[1] - 1))
    keep = (pos < past_len).reshape(pos.shape + (1,) * (old.ndim - pos.ndim))
    return jnp.where(keep, old, new)


def dsa_attention(q, q_idx, w_idx, qpos, k_idx_ctx, fetch_kv, t5_tab):
    n_keys = k_idx_ctx.shape[1]
    k_sel = min(DSA_TOPK, n_keys // 4)
    kpos = jnp.arange(n_keys, dtype=jnp.int32)
    kidx = k_idx_ctx.astype(jnp.float32)
    scale = HEAD_DIM ** -0.5

    def block(qb, qib, wb, pb):
        dots = jnp.einsum('bthd,bsd->bths', qib.astype(jnp.float32), kidx)
        score = jnp.einsum('bth,bths->bts', wb.astype(jnp.float32), jax.nn.relu(dots))
        score = jnp.where(kpos <= pb[..., None], score, NEG)
        _, sel = lax.top_k(score, k_sel)
        kv = fetch_kv(sel)
        rel = pb[..., None] - sel
        s = jnp.einsum('bthd,btkd->bthk', qb, kv[..., 0, :]).astype(jnp.float32) * scale
        s = s + jnp.moveaxis(t5_tab[rel_bucket(rel)], -1, 2).astype(jnp.float32)
        s = jnp.where((rel >= 0)[:, :, None, :], s, NEG)
        p = jax.nn.softmax(s, axis=-1).astype(kv.dtype)
        return jnp.einsum('bthk,btkd->bthd', p, kv[..., 1, :])

    return map_query_blocks(block, (q, q_idx, w_idx, qpos), Q_BLOCK)


def forgetting_attention(q, f_q, qpos, kv_ctx, f_ctx):
    n_keys = kv_ctx.shape[1]
    kpos = jnp.arange(n_keys, dtype=jnp.int32)
    k = kv_ctx[..., 0, :]
    v = kv_ctx[..., 1, :]
    f_k = jnp.moveaxis(f_ctx, 1, 2)
    scale = HEAD_DIM ** -0.5

    def block(qb, fb, pb):
        s = jnp.einsum('bthd,bsd->bhts', qb, k).astype(jnp.float32) * scale
        s = s + jnp.moveaxis(fb, 1, 2)[..., None] - f_k[:, :, None, :]
        s = jnp.where(kpos <= pb[:, None, :, None], s, NEG)
        p = jax.nn.softmax(s, axis=-1).astype(v.dtype)
        return jnp.einsum('bhts,bsd->bthd', p, v)

    return map_query_blocks(block, (q, f_q, qpos), Q_BLOCK)


def moba_attention(q, qpos, kv_ctx, t5_tab):
    B, n_keys = kv_ctx.shape[0], kv_ctx.shape[1]
    n_blk = -(-n_keys // MOBA_BLOCK)
    k_sel = min(MOBA_TOPK, n_blk)
    k_pad = jnp.pad(kv_ctx[..., 0, :].astype(jnp.float32),
                    ((0, 0), (0, n_blk * MOBA_BLOCK - n_keys), (0, 0)))
    k_mean = k_pad.reshape(B, n_blk, MOBA_BLOCK, HEAD_DIM).mean(axis=2)
    blk_ids = jnp.arange(n_blk, dtype=jnp.int32)
    offs = jnp.arange(MOBA_BLOCK, dtype=jnp.int32)
    t5_heads = t5_tab.T
    head_ids = jnp.arange(N_HEADS)[:, None]
    scale = HEAD_DIM ** -0.5

    def block(qb, pb):
        Bq, Tb = pb.shape
        cur = pb // MOBA_BLOCK
        gate = jnp.einsum('bthd,bnd->bthn', qb.astype(jnp.float32), k_mean)
        gate = jnp.where(blk_ids < cur[..., None, None], gate, NEG)
        _, sel = lax.top_k(gate, k_sel)
        sel_ok = jnp.arange(k_sel) < cur[..., None, None]
        pos_sel = (sel[..., None] * MOBA_BLOCK + offs).reshape(Bq, Tb, N_HEADS, k_sel * MOBA_BLOCK)
        ok_sel = jnp.broadcast_to(jnp.repeat(sel_ok, MOBA_BLOCK, axis=-1), pos_sel.shape)
        pos_own = jnp.broadcast_to((cur * MOBA_BLOCK)[..., None, None] + offs,
                                   (Bq, Tb, N_HEADS, MOBA_BLOCK))
        ok_own = pos_own <= pb[..., None, None]
        pos = jnp.concatenate([pos_sel, pos_own], axis=-1)
        ok = jnp.concatenate([ok_sel, ok_own], axis=-1)
        kv = gather_rows(kv_ctx, jnp.clip(pos, 0, n_keys - 1))
        s = jnp.einsum('bthd,bthkd->bthk', qb, kv[..., 0, :]).astype(jnp.float32) * scale
        rel = pb[..., None, None] - pos
        s = s + t5_heads[head_ids, rel_bucket(rel)].astype(jnp.float32)
        s = jnp.where(ok, s, NEG)
        p = jax.nn.softmax(s, axis=-1).astype(kv.dtype)
        return jnp.einsum('bthk,bthkd->bthd', p, kv[..., 1, :])

    return map_query_blocks(block, (q, qpos), MOBA_Q_BLOCK)


def latent_attention(q_lat, q_rope, qpos, lat_ctx, rope_ctx):
    n_keys = lat_ctx.shape[1]
    kpos = jnp.arange(n_keys, dtype=jnp.int32)
    scale = (MLA_NOPE + MLA_ROPE) ** -0.5

    def block(qlb, qrb, pb):
        s = (jnp.einsum('bthc,bsc->bhts', qlb, lat_ctx)
             + jnp.einsum('bthr,bsr->bhts', qrb, rope_ctx)).astype(jnp.float32) * scale
        s = jnp.where(kpos <= pb[:, None, :, None], s, NEG)
        p = jax.nn.softmax(s, axis=-1).astype(lat_ctx.dtype)
        return jnp.einsum('bhts,bsc->bthc', p, lat_ctx)

    return map_query_blocks(block, (q_lat, q_rope, qpos), Q_BLOCK)


def token_mixers(h, l, W, past):
    B, T, D = h.shape
    f32 = jnp.float32
    z = split_columns(_mm(h.reshape(B * T, D), W['w_in'][l]).reshape(B, T, -1), D)
    n_past = 0 if past is None else past[1].shape[1] * PAGE_SIZE
    qpos = jnp.broadcast_to(n_past + jnp.arange(T, dtype=jnp.int32), (B, T))
    q_a = z['q_a'].reshape(B, T, N_HEADS, HEAD_DIM)
    q_idx = z['q_idx'].reshape(B, T, N_IDX_HEADS, IDX_DIM)
    w_idx = z['w_idx'] * N_IDX_HEADS ** -0.5
    a_kv = jnp.stack([z['k_a'], z['v_a']], axis=2)
    a_idx = z['k_idx']
    q_b = z['q_b'].reshape(B, T, N_HEADS, HEAD_DIM)
    b_kv = jnp.stack([z['k_b'], z['v_b']], axis=2)
    b_logf = jax.nn.log_sigmoid(z['f_b'].astype(f32) + W['b_forget'][l].astype(f32)).astype(h.dtype)
    q_c = z['q_c'].reshape(B, T, N_HEADS, HEAD_DIM)
    c_kv = jnp.stack([z['k_c'], z['v_c']], axis=2)
    cq = rmsnorm(z['cq_d'], W['g_cq'][l])
    q_d = _mm(cq.reshape(B * T, -1), W['w_uq'][l]).reshape(B, T, N_HEADS, MLA_NOPE + MLA_ROPE)
    q_rope = apply_rope(q_d[..., MLA_NOPE:], qpos)
    w_uk = W['w_ukv'][l][..., :MLA_NOPE]
    w_uv = W['w_ukv'][l][..., MLA_NOPE:]
    q_lat = jnp.einsum('bthn,chn->bthc', q_d[..., :MLA_NOPE], w_uk)
    d_lat = rmsnorm(z['ckv_d'], W['g_ckv'][l])
    d_rope = apply_rope(z['kr_d'][:, :, None, :], qpos)[:, :, 0, :]

    if past is None:
        def ctx(name, rows):
            return rows

        def fetch_a(pos):
            return gather_rows(a_kv, pos)
    else:
        caches, page_table = past

        def ctx(name, rows):
            return paged_context(caches[name], l, page_table, rows)

        def fetch_a(pos):
            return paged_fetch(caches['a_kv'], l, page_table, a_kv, pos)

    t5_a = W['t5_table'][:, :N_HEADS]
    t5_c = W['t5_table'][:, N_HEADS:]
    o_a = dsa_attention(q_a, q_idx, w_idx, qpos, ctx('a_idx', a_idx), fetch_a, t5_a)
    f_ctx = jnp.cumsum(ctx('b_logf', b_logf).astype(f32), axis=1)
    o_b = forgetting_attention(q_b, f_ctx[:, -T:], qpos, ctx('b_kv', b_kv), f_ctx)
    o_c = moba_attention(q_c, qpos, ctx('c_kv', c_kv), t5_c)
    o_lat = latent_attention(q_lat, q_rope, qpos, ctx('d_lat', d_lat), ctx('d_rope', d_rope))
    o_d = jnp.einsum('bthc,chn->bthn', o_lat, w_uv)
    branches = [o_a, o_b, o_c, o_d]
    gates = jax.nn.sigmoid(z['gates'].reshape(B, T, N_BRANCH, D))
    mixed = 0.0
    for n in range(N_BRANCH):
        y = _mm(branches[n].reshape(B * T, BRANCH_WIDTH), W['w_branch'][l, n]).reshape(B, T, D)
        mixed = mixed + gates[:, :, n, :] * y
    out = _mm(mixed.reshape(B * T, D), W['w_out'][l]).reshape(B, T, D)
    return out, (a_kv, a_idx, b_kv, b_logf, c_kv, d_lat, d_rope)


def peer_ffn(h, l, W):
    B, T, D = h.shape
    n = B * T
    hf = h.reshape(1, n, D)
    q = _mm(h.reshape(n, D), W['w_peer_q'][l]).reshape(n, PEER_HEADS, PEER_KEY_DIM)
    q = q.astype(jnp.float32)
    half = PEER_KEY_DIM // 2
    s1 = jnp.einsum('nhd,kd->nhk', q[..., :half], W['peer_k1'][l].astype(jnp.float32))
    s2 = jnp.einsum('nhd,kd->nhk', q[..., half:], W['peer_k2'][l].astype(jnp.float32))
    v1, i1 = lax.top_k(s1, PEER_TOPK)
    v2, i2 = lax.top_k(s2, PEER_TOPK)
    cand = (v1[..., :, None] + v2[..., None, :]).reshape(n, PEER_HEADS, PEER_TOPK * PEER_TOPK)
    top, ci = lax.top_k(cand, PEER_TOPK)
    expert = (jnp.take_along_axis(i1, ci // PEER_TOPK, axis=-1) * PEER_N_KEYS
              + jnp.take_along_axis(i2, ci % PEER_TOPK, axis=-1))
    g = jax.nn.softmax(top, axis=-1)
    expert = expert.reshape(1, n, PEER_HEADS * PEER_TOPK)
    g = g.reshape(1, n, PEER_HEADS * PEER_TOPK).astype(h.dtype)
    u_tab = W['peer_u']
    v_tab = W['peer_v']

    def block(hb, eb, gb):
        u = u_tab[l, eb]
        a = jax.nn.gelu(jnp.einsum('znd,zned->zne', hb, u), approximate=False)
        v = v_tab[l, eb]
        return jnp.einsum('zne,zned->znd', gb * a, v)

    out = map_query_blocks(block, (hf, expert, g), PEER_BLOCK)
    return out.reshape(B, T, D)


def run_trunk(x, c, past, W):
    per_layer = []
    depth = W['w_ada'].shape[0]
    for l in range(depth):
        mod = _mm(jax.nn.silu(c), W['w_ada'][l]) + W['b_ada'][l]
        sh1, sc1, g1, sh2, sc2, g2 = jnp.split(mod[:, None, :], 6, axis=-1)
        h = rmsnorm(x, W['norm_mix'][l]) * (1.0 + sc1) + sh1
        mix, rows = token_mixers(h, l, W, past)
        x = x + g1 * mix
        h = rmsnorm(x, W['norm_ffn'][l]) * (1.0 + sc2) + sh2
        x = x + g2 * peer_ffn(h, l, W)
        per_layer.append(rows)
    y = rmsnorm(x, W['norm_final'])
    stacked = tuple(jnp.stack([rows[i] for rows in per_layer], axis=0) for i in range(7))
    return y, stacked


def kernel(x_prompt, x_sample, cache_a_kv, cache_a_idx, cache_b_kv, cache_b_logf, cache_c_kv,
           cache_d_latent, cache_d_rope, page_table, c_prompt, c_sample, t5_table, w_ada, b_ada,
           norm_mix, norm_ffn, w_in, b_forget, g_cq, g_ckv, w_uq, w_ukv, w_branch, w_out,
           w_peer_q, peer_k1, peer_k2, peer_u, peer_v, norm_final):
    W = {
        't5_table': t5_table, 'w_ada': w_ada, 'b_ada': b_ada, 'norm_mix': norm_mix,
        'norm_ffn': norm_ffn, 'w_in': w_in, 'b_forget': b_forget, 'g_cq': g_cq, 'g_ckv': g_ckv,
        'w_uq': w_uq, 'w_ukv': w_ukv, 'w_branch': w_branch, 'w_out': w_out,
        'w_peer_q': w_peer_q, 'peer_k1': peer_k1, 'peer_k2': peer_k2, 'peer_u': peer_u,
        'peer_v': peer_v, 'norm_final': norm_final,
    }
    caches = {
        'a_kv': cache_a_kv, 'a_idx': cache_a_idx, 'b_kv': cache_b_kv, 'b_logf': cache_b_logf,
        'c_kv': cache_c_kv, 'd_lat': cache_d_latent, 'd_rope': cache_d_rope,
    }
    y_prompt, rows_p = run_trunk(x_prompt, c_prompt, None, W)
    y_sample, rows_s = run_trunk(x_sample, c_sample, (caches, page_table), W)
    return (y_prompt, y_sample) + tuple(rows_p) + tuple(rows_s)
```

```python
import math
from functools import partial

import jax
import jax.numpy as jnp
from jax import lax
from jax.experimental import pallas as pl
from jax.experimental.pallas import tpu as pltpu

PAGE_SIZE = 128
HEAD_DIM = 64
N_HEADS = 8
N_IDX_HEADS = 16
IDX_DIM = 32
DSA_TOPK = 256
MOBA_BLOCK = 256
MOBA_TOPK = 3
MLA_Q_RANK = 384
MLA_KV_RANK = 128
MLA_NOPE = 64
MLA_ROPE = 32
MLA_V = 64
ROPE_THETA = 10000.0
N_BUCKETS = 32
MAX_DISTANCE = 128
N_BRANCH = 4
BRANCH_WIDTH = N_HEADS * HEAD_DIM
PEER_HEADS = 8
PEER_KEY_DIM = 256
PEER_N_KEYS = 128
PEER_TOPK = 16
PEER_SEL = PEER_HEADS * PEER_TOPK
Q_BLOCK = 128
MOBA_Q_BLOCK = 16
EPS = 1e-6
NEG = -1e30

LANES = 128
VMEM_LIMIT = 56 * 1024 * 1024
TQ = 256
MLA_QW = 256

F32 = jnp.float32
BF16 = jnp.bfloat16

Z_CQ, Z_CKV, Z_KR, Z_FB = 0, 384, 512, 640
Z_PREP_W = 768
Z_QA, Z_KVA, Z_QIDX, Z_KIDX = 768, 1280, 1408, 1920
Z_WIDX = Z_KIDX + IDX_DIM
Z_QB, Z_KVB, Z_QC, Z_KVC = 2048, 2560, 2688, 3200
Z_SMALL = 3328


def _in_offsets(d_model):
    layout = (
        ('q_a', N_HEADS * HEAD_DIM), ('k_a', HEAD_DIM), ('v_a', HEAD_DIM),
        ('q_idx', N_IDX_HEADS * IDX_DIM), ('k_idx', IDX_DIM), ('w_idx', N_IDX_HEADS),
        ('q_b', N_HEADS * HEAD_DIM), ('k_b', HEAD_DIM), ('v_b', HEAD_DIM), ('f_b', N_HEADS),
        ('q_c', N_HEADS * HEAD_DIM), ('k_c', HEAD_DIM), ('v_c', HEAD_DIM),
        ('cq_d', MLA_Q_RANK), ('ckv_d', MLA_KV_RANK), ('kr_d', MLA_ROPE),
        ('gates', N_BRANCH * d_model),
    )
    offs, off = {}, 0
    for name, size in layout:
        offs[name] = (off, size)
        off += size
    return offs


def _cparams(*sem):
    return pltpu.CompilerParams(dimension_semantics=sem, vmem_limit_bytes=VMEM_LIMIT)


def _dot(a, b):
    return jnp.dot(a, b, preferred_element_type=F32)


def _dot_nt(a, b):
    return lax.dot_general(a, b, (((1,), (1,)), ((), ())), preferred_element_type=F32)


def _pick(n, cands):
    for c in cands:
        if n % c == 0:
            return c
    return n


def _mm_kernel(*refs, silu_in, has_bias):
    a_ref, w_ref = refs[0], refs[1]
    o_ref = refs[-1]
    a = a_ref[...]
    if silu_in:
        a = a.astype(F32)
        a = a * jax.nn.sigmoid(a)
    acc = _dot(a.astype(BF16), w_ref[...].astype(BF16))
    if has_bias:
        acc = acc + refs[2][...]
    o_ref[...] = acc.astype(o_ref.dtype)


def _mm(a, w, bias=None, silu_in=False, out_dtype=F32):
    M, K = a.shape
    N = w.shape[1]
    tm = _pick(M, (1024, 512, 256, 128))
    tn = _pick(N, (512, 256, 128))
    in_specs = [pl.BlockSpec((tm, K), lambda i, j: (i, 0)),
                pl.BlockSpec((K, tn), lambda i, j: (0, j))]
    args = [a, w]
    if bias is not None:
        in_specs.append(pl.BlockSpec((1, tn), lambda i, j: (0, j)))
        args.append(bias.reshape(1, N).astype(F32))
    return pl.pallas_call(
        partial(_mm_kernel, silu_in=silu_in, has_bias=bias is not None),
        grid=(M // tm, N // tn),
        in_specs=in_specs,
        out_specs=pl.BlockSpec((tm, tn), lambda i, j: (i, j)),
        out_shape=jax.ShapeDtypeStruct((M, N), out_dtype),
        compiler_params=_cparams("parallel", "parallel"),
    )(*args)


def _norm_kernel(*refs, has_delta, has_mod):
    it = iter(refs)
    x_ref = next(it)
    x = x_ref[...]
    if has_delta:
        d_ref, gt_ref = next(it), next(it)
        x = x + gt_ref[...] * d_ref[...]
    g_ref = next(it)
    if has_mod:
        sc_ref, sh_ref = next(it), next(it)
    if has_delta:
        xo_ref = next(it)
        xo_ref[...] = x
    h_ref = next(it)
    y = x * lax.rsqrt(jnp.mean(x * x, axis=-1, keepdims=True) + EPS) * g_ref[...]
    if has_mod:
        y = y * (1.0 + sc_ref[...]) + sh_ref[...]
    h_ref[...] = y.astype(h_ref.dtype)


def _norm(x, T, g, delta=None, gate=None, sc=None, sh=None, out_dtype=BF16):
    n, D = x.shape
    tr = _pick(n, (256, 128)) if T > 1 else n
    if T > 1:
        assert T % tr == 0

    def row_spec():
        return pl.BlockSpec((tr, D), lambda i: (i, 0))

    def mod_arg(a):
        if T == 1:
            return a, row_spec()
        return a[:, None, :], pl.BlockSpec((None, 1, D), lambda i: (i * tr // T, 0, 0))

    has_delta, has_mod = delta is not None, sc is not None
    args, specs = [x], [row_spec()]
    if has_delta:
        ga, gs = mod_arg(gate)
        args += [delta, ga]
        specs += [row_spec(), gs]
    args.append(g.reshape(1, D))
    specs.append(pl.BlockSpec((1, D), lambda i: (0, 0)))
    if has_mod:
        for a in (sc, sh):
            aa, ss = mod_arg(a)
            args.append(aa)
            specs.append(ss)
    out_shape, out_specs = [], []
    if has_delta:
        out_shape.append(jax.ShapeDtypeStruct((n, D), F32))
        out_specs.append(row_spec())
    out_shape.append(jax.ShapeDtypeStruct((n, D), out_dtype))
    out_specs.append(row_spec())
    outs = pl.pallas_call(
        partial(_norm_kernel, has_delta=has_delta, has_mod=has_mod),
        grid=(n // tr,), in_specs=specs, out_specs=out_specs, out_shape=out_shape,
        compiler_params=_cparams("parallel"),
    )(*args)
    if has_delta:
        return outs[0], outs[1]
    return x, outs[0]


def _prep_kernel(z_ref, gcq_ref, gckv_ref, bf_ref, wq_ref, wuk_ref, c128_ref, s128_ref, cq_ref, sq_ref,
                 logf_ref, dlat_ref, drope_ref, qm_ref, kc_ref):
    nope_w = N_HEADS * MLA_NOPE
    qw = N_HEADS * MLA_QW
    cq = z_ref[:, Z_CQ:Z_CQ + MLA_Q_RANK]
    ckv = z_ref[:, Z_CKV:Z_CKV + MLA_KV_RANK]
    kr = z_ref[:, Z_KR:Z_KR + LANES]
    fb = z_ref[:, Z_FB:Z_FB + LANES]

    def rms(x, g):
        return x * lax.rsqrt(jnp.mean(x * x, axis=-1, keepdims=True) + EPS) * g

    cqn = rms(cq, gcq_ref[...]).astype(BF16)
    qall = _dot(cqn, wq_ref[...])
    q_nope = qall[:, :nope_w].astype(BF16)
    q_r = qall[:, nope_w:nope_w + qw]
    q_rs = qall[:, nope_w + qw:]
    q_lat = _dot(q_nope, wuk_ref[...])
    cos_q = jnp.concatenate([cq_ref[...]] * N_HEADS, axis=1)
    sin_q = jnp.concatenate([sq_ref[...]] * N_HEADS, axis=1)
    scale = (MLA_NOPE + MLA_ROPE) ** -0.5
    qm_ref[...] = ((q_lat + q_r * cos_q + q_rs * sin_q) * scale).astype(BF16)

    dlat = rms(ckv, gckv_ref[...])
    dlat_ref[...] = dlat
    kr_sw = pltpu.roll(kr, LANES - MLA_ROPE, axis=1)
    drope = kr * c128_ref[...] + kr_sw * s128_ref[...]
    drope_ref[...] = drope
    kc_ref[...] = jnp.concatenate([dlat, drope], axis=1).astype(BF16)

    xf = fb + bf_ref[...]
    logf_ref[...] = jnp.minimum(xf, 0.0) - jnp.log1p(jnp.exp(-jnp.abs(xf)))


def _prep(z, T, lw, rope):
    n = z.shape[0]
    tm = _pick(n, (512, 256, 128))
    c128, s128, cq, sq = rope
    if T == 1:
        def tab_spec(w):
            return pl.BlockSpec((1, w), lambda i: (0, 0))
    else:
        assert T % tm == 0
        nt = T // tm

        def tab_spec(w):
            return pl.BlockSpec((tm, w), lambda i: (i % nt, 0))

    def full(a):
        return pl.BlockSpec(a.shape, lambda i: (0,) * a.ndim)

    def rows(w):
        return pl.BlockSpec((tm, w), lambda i: (i, 0))

    return pl.pallas_call(
        _prep_kernel,
        grid=(n // tm,),
        in_specs=[pl.BlockSpec((tm, Z_PREP_W), lambda i: (i, 0)),
                  full(lw['g_cq']), full(lw['g_ckv']), full(lw['b_forget']), full(lw['wq_all']),
                  full(lw['wuk_bd']), tab_spec(LANES), tab_spec(LANES), tab_spec(MLA_QW), tab_spec(MLA_QW)],
        out_specs=[rows(LANES), rows(LANES), rows(LANES), rows(N_HEADS * MLA_QW), rows(MLA_QW)],
        out_shape=[jax.ShapeDtypeStruct((n, LANES), F32), jax.ShapeDtypeStruct((n, LANES), F32),
                   jax.ShapeDtypeStruct((n, LANES), F32),
                   jax.ShapeDtypeStruct((n, N_HEADS * MLA_QW), BF16),
                   jax.ShapeDtypeStruct((n, MLA_QW), BF16)],
        compiler_params=_cparams("parallel"),
    )(z, lw['g_cq'], lw['g_ckv'], lw['b_forget'], lw['wq_all'], lw['wuk_bd'], c128, s128, cq, sq)


CUM_CHUNK = 256


def _cumsum_kernel(x_ref, o_ref):
    T = x_ref.shape[0]
    r = lax.broadcasted_iota(jnp.int32, (CUM_CHUNK, CUM_CHUNK), 0)
    c = lax.broadcasted_iota(jnp.int32, (CUM_CHUNK, CUM_CHUNK), 1)
    tri = jnp.where(r >= c, 1.0, 0.0).astype(BF16)

    def body(i, carry):
        off = pl.multiple_of(i * CUM_CHUNK, CUM_CHUNK)
        x = x_ref[pl.ds(off, CUM_CHUNK), :]
        hi = x.astype(BF16)
        r1 = x - hi.astype(F32)
        mid = r1.astype(BF16)
        lo = (r1 - mid.astype(F32)).astype(BF16)
        f = _dot(tri, hi) + _dot(tri, mid) + _dot(tri, lo) + carry
        o_ref[pl.ds(off, CUM_CHUNK), :] = f
        return f[CUM_CHUNK - 1:CUM_CHUNK, :]

    lax.fori_loop(0, T // CUM_CHUNK, body, jnp.zeros((1, LANES), F32))


def _cumsum(x):
    B, T, _ = x.shape
    assert T % CUM_CHUNK == 0
    spec = pl.BlockSpec((None, T, LANES), lambda b: (b, 0, 0))
    return pl.pallas_call(
        _cumsum_kernel, grid=(B,), in_specs=[spec], out_specs=spec,
        out_shape=jax.ShapeDtypeStruct(x.shape, F32), compiler_params=_cparams("parallel"),
    )(x)


def _softmax_pv(s, v):
    m = jnp.max(s, axis=-1, keepdims=True)
    p = jnp.exp(s - m)
    l = jnp.sum(p, axis=-1, keepdims=True)
    return _dot(p.astype(BF16), v) / l


def _causal(qi, T):
    row = lax.broadcasted_iota(jnp.int32, (TQ, T), 0) + qi * TQ
    col = lax.broadcasted_iota(jnp.int32, (TQ, T), 1)
    return row, col


def _count(mask):
    return jnp.sum(jnp.where(mask, 1.0, 0.0), axis=-1, keepdims=True)


def _fox_kernel(q_ref, k_ref, v_ref, fq_ref, fk_ref, o_ref):
    qi = pl.program_id(1)
    k, v = k_ref[...], v_ref[...]
    row, col = _causal(qi, k.shape[0])
    causal = col <= row

    def body(h, _):
        s = _dot_nt(q_ref[h], k) + fq_ref[h] - fk_ref[h]
        o_ref[h] = _softmax_pv(jnp.where(causal, s, NEG), v)
        return 0

    lax.fori_loop(0, N_HEADS, body, 0)


def _mla_kernel(q_ref, kc_ref, wuv_ref, o_ref):
    qi = pl.program_id(1)
    kc = kc_ref[...]
    v = kc[:, :MLA_KV_RANK]
    row, col = _causal(qi, kc.shape[0])
    causal = col <= row

    def body(h, _):
        s = _dot_nt(q_ref[h], kc)
        o_lat = _softmax_pv(jnp.where(causal, s, NEG), v)
        o_ref[h] = _dot(o_lat.astype(BF16), wuv_ref[h])
        return 0

    lax.fori_loop(0, N_HEADS, body, 0)


def _biased_logits(q, k, h, qi, dt_ref, c_ref, s_ref):
    off = pl.multiple_of(qi * TQ, TQ)
    s_ref[...] = _dot_nt(q, k) + c_ref[h]
    s_ref[:, pl.ds(off, TQ)] += dt_ref[0, h]

    @pl.when(qi > 0)
    def _():
        s_ref[:, pl.ds(off - TQ, TQ)] += dt_ref[1, h]

    return s_ref[...]


def _moba_kernel(q_ref, k_ref, v_ref, kf_ref, dt_ref, c_ref, o_ref, s_ref, *, k_sel):
    qi = pl.program_id(1)
    k, v = k_ref[...], v_ref[...]
    T = k.shape[0]
    nb = T // MOBA_BLOCK
    km = jnp.mean(kf_ref[...].reshape(nb, MOBA_BLOCK, HEAD_DIM), axis=1)
    km = jnp.concatenate([km, jnp.zeros((LANES - nb, HEAD_DIM), F32)], axis=0).astype(BF16)
    row, col = _causal(qi, T)
    own = (col >= qi * TQ) & (col <= row)
    lane = lax.broadcasted_iota(jnp.int32, (TQ, LANES), 1)
    eb = lax.broadcasted_iota(jnp.int32, (LANES, T), 0)
    es = lax.broadcasted_iota(jnp.int32, (LANES, T), 1)
    expand = jnp.where(es // MOBA_BLOCK == eb, 1.0, 0.0).astype(BF16)

    def body(h, _):
        q = q_ref[h]
        gate = _dot_nt(q, km)
        rank = jnp.zeros((TQ, LANES), F32)
        for m in range(nb):
            gm = gate[:, m:m + 1]
            beats = (gm > gate) | ((gm == gate) & (lane > m))
            rank = rank + jnp.where(beats, jnp.where(m < qi, 1.0, 0.0), 0.0)
        selb = jnp.where((rank < k_sel) & (lane < qi), 1.0, 0.0).astype(BF16)
        picked = _dot(selb, expand) > 0.5
        s = _biased_logits(q, k, h, qi, dt_ref, c_ref, s_ref)
        o_ref[h] = _softmax_pv(jnp.where(picked | own, s, NEG), v)
        return 0

    lax.fori_loop(0, N_HEADS, body, 0)


def _sortable(x):
    b = pltpu.bitcast(x, jnp.int32)
    return b ^ ((b >> 31) & jnp.int32(0x7FFFFFFF))


def _topk_mask(score, col, k_sel, n_idx_bits):
    key = _sortable(score)
    kf = float(k_sel)
    nonneg = _count(key >= 0) >= kf
    prefix = jnp.where(nonneg, jnp.int32(0), jnp.int32(-2 ** 31))

    def vbody(i, prefix):
        cand = prefix | (jnp.int32(1) << (30 - i))
        return jnp.where(_count(key >= cand) >= kf, cand, prefix)

    thr = lax.fori_loop(0, 31, vbody, prefix)
    above = key > thr
    tie = key == thr
    need = kf - _count(above)

    def ibody(i, p):
        cand = p | (jnp.int32(1) << (n_idx_bits - 1 - i))
        return jnp.where(_count(tie & (col < cand)) < need, cand, p)

    last = lax.fori_loop(0, n_idx_bits, ibody, jnp.zeros_like(thr))
    return above | (tie & (col <= last))


def _dsa_kernel(q_ref, qi_ref, w_ref, kidx_ref, k_ref, v_ref, dt_ref, c_ref, o_ref, s_ref, sc_ref, *, k_sel):
    qi = pl.program_id(1)
    k, v, kidx = k_ref[...], v_ref[...], kidx_ref[...]
    T = k.shape[0]
    row, col = _causal(qi, T)
    causal = col <= row

    sc_ref[...] = jnp.zeros_like(sc_ref)

    def ibody(h, _):
        sc_ref[...] += w_ref[h] * jnp.maximum(_dot_nt(qi_ref[h], kidx), 0.0)
        return 0

    lax.fori_loop(0, N_IDX_HEADS, ibody, 0)
    score = jnp.where(causal, sc_ref[...], NEG)
    mask = _topk_mask(score, col, k_sel, max(1, (T - 1).bit_length())) & causal

    def body(h, _):
        s = _biased_logits(q_ref[h], k, h, qi, dt_ref, c_ref, s_ref)
        o_ref[h] = _softmax_pv(jnp.where(mask, s, NEG), v)
        return 0

    lax.fori_loop(0, N_HEADS, body, 0)


def _heads_spec(nh, w):
    return pl.BlockSpec((None, nh, TQ, w), lambda b, i: (b, 0, i, 0))


def _ctx_spec(T, w):
    return pl.BlockSpec((None, T, w), lambda b, i: (b, 0, 0))


def _full_spec(a):
    return pl.BlockSpec(a.shape, lambda b, i: (0,) * a.ndim)


def _attn_call(kern, B, T, in_specs, args, scratch=()):
    return pl.pallas_call(
        kern, grid=(B, T // TQ), in_specs=in_specs,
        out_specs=_heads_spec(N_HEADS, HEAD_DIM),
        out_shape=jax.ShapeDtypeStruct((B, N_HEADS, T, HEAD_DIM), F32),
        scratch_shapes=list(scratch),
        compiler_params=_cparams("parallel", "arbitrary"),
    )(*args)


def _to_heads(x, B, T, nh, d, scale=1.0):
    return (x.reshape(B, T, nh, d) * scale).astype(BF16).transpose(0, 2, 1, 3)


def _from_heads(o):
    B, nh, T, d = o.shape
    return o.transpose(0, 2, 1, 3).reshape(B * T, nh * d).astype(BF16)


def _prompt_mixers(z, logf, q_mla, kc, lw, bias_a, bias_c, B, T):
    assert T % TQ == 0 and TQ == MOBA_BLOCK and MAX_DISTANCE <= TQ
    scale = HEAD_DIM ** -0.5

    def ctx(lo, w):
        return z[:, lo:lo + w].reshape(B, T, w)

    sblock = pltpu.VMEM((TQ, T), F32)
    q_a = _to_heads(z[:, Z_QA:Z_QA + BRANCH_WIDTH], B, T, N_HEADS, HEAD_DIM, scale)
    q_idx = _to_heads(z[:, Z_QIDX:Z_QIDX + N_IDX_HEADS * IDX_DIM], B, T, N_IDX_HEADS, IDX_DIM)
    w_idx = (z[:, Z_WIDX:Z_WIDX + N_IDX_HEADS] * N_IDX_HEADS ** -0.5).reshape(B, T, N_IDX_HEADS)
    w_idx = w_idx.transpose(0, 2, 1)[..., None]
    k_idx = ctx(Z_KIDX, IDX_DIM).astype(BF16)
    ka = ctx(Z_KVA, HEAD_DIM).astype(BF16)
    va = ctx(Z_KVA + HEAD_DIM, HEAD_DIM).astype(BF16)
    dt_a, c_a = bias_a
    o_a = _attn_call(
        partial(_dsa_kernel, k_sel=min(DSA_TOPK, T // 4)), B, T,
        [_heads_spec(N_HEADS, HEAD_DIM), _heads_spec(N_IDX_HEADS, IDX_DIM), _heads_spec(N_IDX_HEADS, 1),
         _ctx_spec(T, IDX_DIM), _ctx_spec(T, HEAD_DIM), _ctx_spec(T, HEAD_DIM), _full_spec(dt_a), _full_spec(c_a)],
        (q_a, q_idx, w_idx, k_idx, ka, va, dt_a, c_a), scratch=(sblock, sblock))
    q_b = _to_heads(z[:, Z_QB:Z_QB + BRANCH_WIDTH], B, T, N_HEADS, HEAD_DIM, scale)
    kb = ctx(Z_KVB, HEAD_DIM).astype(BF16)
    vb = ctx(Z_KVB + HEAD_DIM, HEAD_DIM).astype(BF16)
    f_cum = _cumsum(logf.reshape(B, T, LANES))[:, :, :N_HEADS].transpose(0, 2, 1)
    o_b = _attn_call(
        _fox_kernel, B, T,
        [_heads_spec(N_HEADS, HEAD_DIM), _ctx_spec(T, HEAD_DIM), _ctx_spec(T, HEAD_DIM), _heads_spec(N_HEADS, 1),
         pl.BlockSpec((None, N_HEADS, 1, T), lambda b, i: (b, 0, 0, 0))],
        (q_b, kb, vb, f_cum[..., None], f_cum[:, :, None, :]))
    q_c = _to_heads(z[:, Z_QC:Z_QC + BRANCH_WIDTH], B, T, N_HEADS, HEAD_DIM, scale)
    kc_f = ctx(Z_KVC, HEAD_DIM)
    vc = ctx(Z_KVC + HEAD_DIM, HEAD_DIM).astype(BF16)
    dt_c, c_c = bias_c
    o_c = _attn_call(
        partial(_moba_kernel, k_sel=min(MOBA_TOPK, T // MOBA_BLOCK)), B, T,
        [_heads_spec(N_HEADS, HEAD_DIM), _ctx_spec(T, HEAD_DIM), _ctx_spec(T, HEAD_DIM), _ctx_spec(T, HEAD_DIM),
         _full_spec(dt_c), _full_spec(c_c)],
        (q_c, kc_f.astype(BF16), vc, kc_f, dt_c, c_c), scratch=(sblock,))
    q_d = q_mla.reshape(B, T, N_HEADS, MLA_QW).transpose(0, 2, 1, 3)
    o_d = _attn_call(
        _mla_kernel, B, T,
        [_heads_spec(N_HEADS, MLA_QW), _ctx_spec(T, MLA_QW), _full_spec(lw['w_uv'])],
        (q_d, kc.reshape(B, T, MLA_QW), lw['w_uv']))
    return jnp.stack([_from_heads(o) for o in (o_a, o_b, o_c, o_d)], axis=0)


def _gate_mix_kernel(h_ref, g0_ref, g1_ref, g2_ref, g3_ref, o_ref, wb_ref, out_ref):
    h = h_ref[...]
    acc = None
    for n, g_ref in enumerate((g0_ref, g1_ref, g2_ref, g3_ref)):
        gate = jax.nn.sigmoid(_dot(h, g_ref[...]))
        term = gate * _dot(o_ref[n], wb_ref[n])
        acc = term if acc is None else acc + term
    out_ref[...] = acc.astype(out_ref.dtype)


def _gate_mix(h, w_gates, o, w_branch):
    n, D = h.shape
    tm = _pick(n, (1024, 512, 256, 128))
    td = _pick(D, (256, 128))
    nd = D // td
    gate_specs = [pl.BlockSpec((D, td), partial(lambda i, j, b: (0, b * nd + j), b=b)) for b in range(N_BRANCH)]
    return pl.pallas_call(
        _gate_mix_kernel,
        grid=(n // tm, nd),
        in_specs=[pl.BlockSpec((tm, D), lambda i, j: (i, 0))] + gate_specs + [
            pl.BlockSpec((N_BRANCH, tm, BRANCH_WIDTH), lambda i, j: (0, i, 0)),
            pl.BlockSpec((N_BRANCH, BRANCH_WIDTH, td), lambda i, j: (0, 0, j))],
        out_specs=pl.BlockSpec((tm, td), lambda i, j: (i, j)),
        out_shape=jax.ShapeDtypeStruct((n, D), BF16),
        compiler_params=_cparams("parallel", "parallel"),
    )(h, w_gates, w_gates, w_gates, w_gates, o, w_branch)


def _top16(s):
    R, tn = s.shape
    rid = lax.broadcasted_iota(jnp.int32, (R, tn), 0)
    kid = lax.broadcasted_iota(jnp.int32, (PEER_TOPK, tn), 0)

    def body(k, carry):
        s, vals, idxs = carry
        m = jnp.max(s, axis=0, keepdims=True)
        i = jnp.min(jnp.where(s == m, rid, R), axis=0, keepdims=True)
        vals = jnp.where(kid == k, m, vals)
        idxs = jnp.where(kid == k, i, idxs)
        return jnp.where(rid == i, -jnp.inf, s), vals, idxs

    _, vals, idxs = lax.fori_loop(
        0, PEER_TOPK, body, (s, jnp.zeros((PEER_TOPK, tn), F32), jnp.zeros((PEER_TOPK, tn), jnp.int32)))
    return vals, idxs


def _peer_select_kernel(q_ref, k1_ref, k2_ref, e1_ref, e2_ref, g_ref):
    half = PEER_KEY_DIM // 2
    k1, k2 = k1_ref[...], k2_ref[...]

    def pick(idx, table):
        out = jnp.zeros_like(idx)
        for j in range(PEER_TOPK):
            out = out + jnp.where(idx == j, table[j:j + 1, :], 0)
        return out

    def body(h, _):
        off = pl.multiple_of(h * PEER_KEY_DIM, PEER_KEY_DIM)
        q1 = q_ref[:, pl.ds(off, half)].astype(BF16)
        q2 = q_ref[:, pl.ds(off + half, half)].astype(BF16)
        v1, i1 = _top16(_dot_nt(k1, q1))
        v2, i2 = _top16(_dot_nt(k2, q2))
        cand = jnp.concatenate([v1[a:a + 1, :] + v2 for a in range(PEER_TOPK)], axis=0)
        top, ci = _top16(cand)
        e = jnp.exp(top - jnp.max(top, axis=0, keepdims=True))
        rows = pl.ds(pl.multiple_of(h * PEER_TOPK, PEER_TOPK), PEER_TOPK)
        e1_ref[rows, :] = pick(ci // PEER_TOPK, i1)
        e2_ref[rows, :] = pick(ci % PEER_TOPK, i2)
        g_ref[rows, :] = e / jnp.sum(e, axis=0, keepdims=True)
        return 0

    lax.fori_loop(0, PEER_HEADS, body, 0)


def _peer_select(q, k1, k2):
    n = q.shape[0]
    tn = _pick(n, (256, 128))
    out_spec = pl.BlockSpec((PEER_SEL, tn), lambda i: (0, i))
    kspec = pl.BlockSpec(k1.shape, lambda i: (0, 0))
    return pl.pallas_call(
        _peer_select_kernel, grid=(n // tn,),
        in_specs=[pl.BlockSpec((tn, PEER_HEADS * PEER_KEY_DIM), lambda i: (i, 0)), kspec, kspec],
        out_specs=[out_spec] * 3,
        out_shape=[jax.ShapeDtypeStruct((PEER_SEL, n), jnp.int32)] * 2 + [jax.ShapeDtypeStruct((PEER_SEL, n), F32)],
        compiler_params=_cparams("parallel"),
    )(q, k1, k2)


PEER_G_TOKENS = 32


def _peer_gates_kernel(e1_ref, e2_ref, g_ref, o_ref):
    shape = (PEER_G_TOKENS, PEER_N_KEYS, PEER_SEL)
    rid = lax.broadcasted_iota(jnp.int32, shape, 1)
    p1 = jnp.where(e1_ref[...][:, None, :] == rid, 1.0, 0.0).astype(BF16)
    gp2 = jnp.where(e2_ref[...][:, None, :] == rid, g_ref[...][:, None, :], 0.0).astype(BF16)
    o_ref[...] = jnp.einsum('nrk,nck->nrc', p1, gp2, preferred_element_type=F32).astype(o_ref.dtype)


def _peer_gates(e1, e2, g):
    n = e1.shape[0]
    spec = pl.BlockSpec((PEER_G_TOKENS, PEER_SEL), lambda i: (i, 0))
    return pl.pallas_call(
        _peer_gates_kernel, grid=(n // PEER_G_TOKENS,),
        in_specs=[spec, spec, spec],
        out_specs=pl.BlockSpec((PEER_G_TOKENS, PEER_N_KEYS, PEER_N_KEYS), lambda i: (i, 0, 0)),
        out_shape=jax.ShapeDtypeStruct((n, PEER_N_KEYS, PEER_N_KEYS), BF16),
        compiler_params=_cparams("parallel"),
    )(e1, e2, g)


def _peer_dense_kernel(h_ref, u_ref, g_ref, v_ref, o_ref):
    @pl.when(pl.program_id(1) == 0)
    def _():
        o_ref[...] = jnp.zeros_like(o_ref)

    a = _dot_nt(h_ref[...], u_ref[...])
    act = 0.5 * a * (1.0 + lax.erf(a * (0.5 ** 0.5)))
    w = (g_ref[...].astype(F32) * act).astype(BF16)
    o_ref[...] += _dot(w, v_ref[...])


def _peer_dense(h, u, g, v):
    n, D = h.shape
    E = u.shape[0]
    tn = _pick(n, (512, 256, 128))
    te = _pick(E, (256, 128))
    return pl.pallas_call(
        _peer_dense_kernel, grid=(n // tn, E // te),
        in_specs=[pl.BlockSpec((tn, D), lambda i, j: (i, 0)), pl.BlockSpec((te, D), lambda i, j: (j, 0)),
                  pl.BlockSpec((tn, te), lambda i, j: (i, j)), pl.BlockSpec((te, D), lambda i, j: (j, 0))],
        out_specs=pl.BlockSpec((tn, D), lambda i, j: (i, 0)),
        out_shape=jax.ShapeDtypeStruct((n, D), F32),
        compiler_params=_cparams("parallel", "arbitrary"),
    )(h, u, g, v)


def _peer(h, lw):
    q = _mm(h, lw['w_peer_q'])
    e1, e2, g = _peer_select(q, lw['peer_k1'], lw['peer_k2'])
    grid = _peer_gates(e1.T, e2.T, g.T)
    return _peer_dense(h, lw['peer_u'], grid.reshape(h.shape[0], PEER_N_KEYS * PEER_N_KEYS), lw['peer_v'])


def _rel_bucket(rel):
    n_exact = N_BUCKETS // 2
    relf = jnp.maximum(rel, 1).astype(F32)
    large = n_exact + (jnp.log(relf / n_exact) / math.log(MAX_DISTANCE / n_exact)
                       * (N_BUCKETS - n_exact)).astype(jnp.int32)
    large = jnp.minimum(large, N_BUCKETS - 1)
    return jnp.where(rel < n_exact, jnp.maximum(rel, 0), large)


def _map_query_blocks(fn, arrays, blk):
    n_q = arrays[0].shape[1]
    blk = min(blk, n_q)
    n_blk = -(-n_q // blk)
    pad = n_blk * blk - n_q

    def to_blocks(a):
        a = jnp.pad(a, [(0, 0), (0, pad)] + [(0, 0)] * (a.ndim - 2))
        a = a.reshape((a.shape[0], n_blk, blk) + a.shape[2:])
        return jnp.moveaxis(a, 1, 0)

    out = lax.map(lambda xs: fn(*xs), tuple(to_blocks(a) for a in arrays))
    out = jnp.moveaxis(out, 0, 1)
    out = out.reshape((out.shape[0], n_blk * blk) + out.shape[3:])
    return out[:, :n_q]


def _gather_rows(rows, pos):
    return jax.vmap(lambda r, p: r[p])(rows, pos)


def _paged_context(cache, l, page_table, new_rows):
    past = cache[l, page_table]
    past = past.reshape((past.shape[0], past.shape[1] * past.shape[2]) + past.shape[3:])
    return jnp.concatenate([past.astype(new_rows.dtype), new_rows], axis=1)


def _paged_fetch(cache, l, page_table, new_rows, pos):
    past_len = page_table.shape[1] * PAGE_SIZE
    pp = jnp.minimum(pos, past_len - 1)
    flat = pp.reshape(pp.shape[0], -1)
    page = jnp.take_along_axis(page_table, flat // PAGE_SIZE, axis=1).reshape(pos.shape)
    old = cache[l, page, pp % PAGE_SIZE].astype(new_rows.dtype)
    new = _gather_rows(new_rows, jnp.clip(pos - past_len, 0, new_rows.shape[1] - 1))
    keep = (pos < past_len).reshape(pos.shape + (1,) * (old.ndim - pos.ndim))
    return jnp.where(keep, old, new)


def _dsa_paged(q, q_idx, w_idx, qpos, k_idx_ctx, fetch_kv, t5_tab):
    n_keys = k_idx_ctx.shape[1]
    k_sel = min(DSA_TOPK, n_keys // 4)
    kpos = jnp.arange(n_keys, dtype=jnp.int32)
    kidx = k_idx_ctx.astype(F32)
    scale = HEAD_DIM ** -0.5

    def block(qb, qib, wb, pb):
        dots = jnp.einsum('bthd,bsd->bths', qib.astype(F32), kidx)
        score = jnp.einsum('bth,bths->bts', wb.astype(F32), jax.nn.relu(dots))
        score = jnp.where(kpos <= pb[..., None], score, NEG)
        _, sel = lax.top_k(score, k_sel)
        kv = fetch_kv(sel)
        rel = pb[..., None] - sel
        s = jnp.einsum('bthd,btkd->bthk', qb, kv[..., 0, :]).astype(F32) * scale
        s = s + jnp.moveaxis(t5_tab[_rel_bucket(rel)], -1, 2).astype(F32)
        s = jnp.where((rel >= 0)[:, :, None, :], s, NEG)
        p = jax.nn.softmax(s, axis=-1).astype(kv.dtype)
        return jnp.einsum('bthk,btkd->bthd', p, kv[..., 1, :])

    return _map_query_blocks(block, (q, q_idx, w_idx, qpos), Q_BLOCK)


def _fox_paged(q, f_q, qpos, kv_ctx, f_ctx):
    n_keys = kv_ctx.shape[1]
    kpos = jnp.arange(n_keys, dtype=jnp.int32)
    k = kv_ctx[..., 0, :]
    v = kv_ctx[..., 1, :]
    f_k = jnp.moveaxis(f_ctx, 1, 2)
    scale = HEAD_DIM ** -0.5

    def block(qb, fb, pb):
        s = jnp.einsum('bthd,bsd->bhts', qb, k).astype(F32) * scale
        s = s + jnp.moveaxis(fb, 1, 2)[..., None] - f_k[:, :, None, :]
        s = jnp.where(kpos <= pb[:, None, :, None], s, NEG)
        p = jax.nn.softmax(s, axis=-1).astype(v.dtype)
        return jnp.einsum('bhts,bsd->bthd', p, v)

    return _map_query_blocks(block, (q, f_q, qpos), Q_BLOCK)


def _moba_paged(q, qpos, kv_ctx, t5_tab):
    B, n_keys = kv_ctx.shape[0], kv_ctx.shape[1]
    n_blk = -(-n_keys // MOBA_BLOCK)
    k_sel = min(MOBA_TOPK, n_blk)
    k_pad = jnp.pad(kv_ctx[..., 0, :].astype(F32), ((0, 0), (0, n_blk * MOBA_BLOCK - n_keys), (0, 0)))
    k_mean = k_pad.reshape(B, n_blk, MOBA_BLOCK, HEAD_DIM).mean(axis=2)
    blk_ids = jnp.arange(n_blk, dtype=jnp.int32)
    offs = jnp.arange(MOBA_BLOCK, dtype=jnp.int32)
    t5_heads = t5_tab.T
    head_ids = jnp.arange(N_HEADS)[:, None]
    scale = HEAD_DIM ** -0.5

    def block(qb, pb):
        Bq, Tb = pb.shape
        cur = pb // MOBA_BLOCK
        gate = jnp.einsum('bthd,bnd->bthn', qb.astype(F32), k_mean)
        gate = jnp.where(blk_ids < cur[..., None, None], gate, NEG)
        _, sel = lax.top_k(gate, k_sel)
        sel_ok = jnp.arange(k_sel) < cur[..., None, None]
        pos_sel = (sel[..., None] * MOBA_BLOCK + offs).reshape(Bq, Tb, N_HEADS, k_sel * MOBA_BLOCK)
        ok_sel = jnp.broadcast_to(jnp.repeat(sel_ok, MOBA_BLOCK, axis=-1), pos_sel.shape)
        pos_own = jnp.broadcast_to((cur * MOBA_BLOCK)[..., None, None] + offs, (Bq, Tb, N_HEADS, MOBA_BLOCK))
        ok_own = pos_own <= pb[..., None, None]
        pos = jnp.concatenate([pos_sel, pos_own], axis=-1)
        ok = jnp.concatenate([ok_sel, ok_own], axis=-1)
        kv = _gather_rows(kv_ctx, jnp.clip(pos, 0, n_keys - 1))
        s = jnp.einsum('bthd,bthkd->bthk', qb, kv[..., 0, :]).astype(F32) * scale
        rel = pb[..., None, None] - pos
        s = s + t5_heads[head_ids, _rel_bucket(rel)].astype(F32)
        s = jnp.where(ok, s, NEG)
        p = jax.nn.softmax(s, axis=-1).astype(kv.dtype)
        return jnp.einsum('bthk,bthkd->bthd', p, kv[..., 1, :])

    return _map_query_blocks(block, (q, qpos), MOBA_Q_BLOCK)


def _mla_paged(q_lat, q_rope, qpos, lat_ctx, rope_ctx):
    n_keys = lat_ctx.shape[1]
    kpos = jnp.arange(n_keys, dtype=jnp.int32)

    def block(qlb, qrb, pb):
        s = (jnp.einsum('bthc,bsc->bhts', qlb, lat_ctx)
             + jnp.einsum('bthr,bsr->bhts', qrb, rope_ctx)).astype(F32)
        s = jnp.where(kpos <= pb[:, None, :, None], s, NEG)
        p = jax.nn.softmax(s, axis=-1).astype(lat_ctx.dtype)
        return jnp.einsum('bhts,bsc->bthc', p, lat_ctx)

    return _map_query_blocks(block, (q_lat, q_rope, qpos), Q_BLOCK)


def _paged_mixers(z, rows, q_mla, l, lw, t5_table, caches, page_table, B, T):
    a_kv, a_idx, b_kv, b_logf, c_kv, d_lat, d_rope = rows
    n_past = page_table.shape[1] * PAGE_SIZE
    qpos = jnp.broadcast_to(n_past + jnp.arange(T, dtype=jnp.int32), (B, T))

    def heads(lo, nh, d):
        return z[:, lo:lo + nh * d].reshape(B, T, nh, d)

    def ctx(name, new_rows):
        return _paged_context(caches[name], l, page_table, new_rows)

    def fetch_a(pos):
        return _paged_fetch(caches['a_kv'], l, page_table, a_kv, pos)

    w_idx = z[:, Z_WIDX:Z_WIDX + N_IDX_HEADS].reshape(B, T, N_IDX_HEADS) * N_IDX_HEADS ** -0.5
    o_a = _dsa_paged(heads(Z_QA, N_HEADS, HEAD_DIM), heads(Z_QIDX, N_IDX_HEADS, IDX_DIM), w_idx, qpos,
                     ctx('a_idx', a_idx), fetch_a, t5_table[:, :N_HEADS])
    f_ctx = jnp.cumsum(ctx('b_logf', b_logf).astype(F32), axis=1)
    o_b = _fox_paged(heads(Z_QB, N_HEADS, HEAD_DIM), f_ctx[:, -T:], qpos, ctx('b_kv', b_kv), f_ctx)
    o_c = _moba_paged(heads(Z_QC, N_HEADS, HEAD_DIM), qpos, ctx('c_kv', c_kv), t5_table[:, N_HEADS:])
    qm = q_mla.reshape(B, T, N_HEADS, MLA_QW).astype(F32)
    o_lat = _mla_paged(qm[..., :MLA_KV_RANK], qm[..., MLA_KV_RANK:MLA_KV_RANK + MLA_ROPE], qpos,
                       ctx('d_lat', d_lat), ctx('d_rope', d_rope))
    o_d = jnp.einsum('bthc,hcn->bthn', o_lat, lw['w_uv'].astype(F32))
    return jnp.stack([o.reshape(B * T, BRANCH_WIDTH).astype(BF16) for o in (o_a, o_b, o_c, o_d)], axis=0)


def _layer_weights(l, W, D):
    offs = _in_offsets(D)
    w_in = W['w_in'][l]

    def seg(name):
        lo, size = offs[name]
        return w_in[:, lo:lo + size]

    def zeros(k):
        return jnp.zeros((D, k), F32)

    half = MLA_ROPE // 2
    kr = seg('kr_d')
    kr_sw = jnp.concatenate([kr[:, half:], kr[:, :half]], axis=1)
    w_small = jnp.concatenate([
        seg('cq_d'), seg('ckv_d'), kr, kr_sw, zeros(LANES - 2 * MLA_ROPE), seg('f_b'), zeros(LANES - N_HEADS),
        seg('q_a'), seg('k_a'), seg('v_a'), seg('q_idx'), seg('k_idx'), seg('w_idx'),
        zeros(LANES - IDX_DIM - N_IDX_HEADS), seg('q_b'), seg('k_b'), seg('v_b'), seg('q_c'), seg('k_c'), seg('v_c'),
    ], axis=1).astype(BF16)
    assert w_small.shape[1] == Z_SMALL

    dq = MLA_NOPE + MLA_ROPE
    w_uq = W['w_uq'][l].reshape(MLA_Q_RANK, N_HEADS, dq)
    w_nope = w_uq[:, :, :MLA_NOPE].reshape(MLA_Q_RANK, N_HEADS * MLA_NOPE)
    rope_w = w_uq[:, :, MLA_NOPE:]
    rope_sw = jnp.concatenate([rope_w[..., half:], rope_w[..., :half]], axis=-1)

    def widen(r):
        wide = jnp.zeros((MLA_Q_RANK, N_HEADS, MLA_QW), F32).at[:, :, MLA_KV_RANK:MLA_KV_RANK + MLA_ROPE].set(r)
        return wide.reshape(MLA_Q_RANK, N_HEADS * MLA_QW)

    wq_all = jnp.concatenate([w_nope, widen(rope_w), widen(rope_sw)], axis=1).astype(BF16)
    w_ukv = W['w_ukv'][l]
    w_uk = jnp.transpose(w_ukv[..., :MLA_NOPE], (1, 2, 0))
    w_uk = jnp.pad(w_uk, ((0, 0), (0, 0), (0, MLA_QW - MLA_KV_RANK)))
    wuk_bd = jnp.einsum('hnc,hg->hngc', w_uk, jnp.eye(N_HEADS, dtype=F32))
    wuk_bd = wuk_bd.reshape(N_HEADS * MLA_NOPE, N_HEADS * MLA_QW).astype(BF16)
    w_uv = jnp.transpose(w_ukv[..., MLA_NOPE:], (1, 0, 2)).astype(BF16)

    return {
        'w_small': w_small, 'w_gates': seg('gates').astype(BF16),
        'wq_all': wq_all, 'wuk_bd': wuk_bd, 'w_uv': w_uv,
        'g_cq': W['g_cq'][l].reshape(1, -1), 'g_ckv': W['g_ckv'][l].reshape(1, -1),
        'b_forget': jnp.pad(W['b_forget'][l], (0, LANES - N_HEADS)).reshape(1, LANES),
        'w_branch': W['w_branch'][l].astype(BF16), 'w_out': W['w_out'][l].astype(BF16),
        'w_peer_q': W['w_peer_q'][l].astype(BF16),
        'peer_k1': W['peer_k1'][l].astype(BF16), 'peer_k2': W['peer_k2'][l].astype(BF16),
        'peer_u': W['peer_u'][l].astype(BF16), 'peer_v': W['peer_v'][l].astype(BF16),
    }


def _rope_tables(pos):
    half = MLA_ROPE // 2
    inv_freq = ROPE_THETA ** (-jnp.arange(half, dtype=F32) / half)
    ang = pos[:, None].astype(F32) * inv_freq
    cos, sin = jnp.cos(ang), jnp.sin(ang)
    pad = jnp.zeros((pos.shape[0], LANES - MLA_ROPE), F32)
    c128 = jnp.concatenate([cos, cos, pad], axis=1)
    s128 = jnp.concatenate([-sin, sin, pad], axis=1)
    lead = jnp.zeros((pos.shape[0], MLA_KV_RANK), F32)
    return c128, s128, jnp.concatenate([lead, c128], axis=1), jnp.concatenate([lead, s128], axis=1)


def _bias_tiles(t5):
    i = jnp.arange(TQ, dtype=jnp.int32)
    rel = jnp.arange(2, dtype=jnp.int32)[:, None, None] * TQ + i[None, :, None] - i[None, None, :]
    far = t5[N_BUCKETS - 1]
    tiles = jnp.where((rel >= 0)[..., None], t5[_rel_bucket(rel)] - far, 0.0)
    return jnp.transpose(tiles, (0, 3, 1, 2)), far.reshape(-1, 1, 1)


def _run_trunk(x, mods, past, W, LW, t5_table):
    B, T, D = x.shape
    n = B * T
    depth = len(LW)
    n_past = 0 if past is None else past[1].shape[1] * PAGE_SIZE
    rope = _rope_tables(n_past + jnp.arange(T, dtype=jnp.int32))
    if past is None:
        bias_a = _bias_tiles(t5_table[:, :N_HEADS])
        bias_c = _bias_tiles(t5_table[:, N_HEADS:])
    xf = x.reshape(n, D)
    delta = gate = None
    per_layer = []
    for l in range(depth):
        lw = LW[l]
        sh1, sc1, g1, sh2, sc2, g2 = jnp.split(mods[l], 6, axis=-1)
        xf, h = _norm(xf, T, W['norm_mix'][l], delta, gate, sc1, sh1)
        z = _mm(h, lw['w_small'])
        logf, d_lat, d_rope, q_mla, kc = _prep(z, T, lw, rope)
        rows = (z[:, Z_KVA:Z_KVA + 2 * HEAD_DIM].reshape(B, T, 2, HEAD_DIM),
                z[:, Z_KIDX:Z_KIDX + IDX_DIM].reshape(B, T, IDX_DIM),
                z[:, Z_KVB:Z_KVB + 2 * HEAD_DIM].reshape(B, T, 2, HEAD_DIM),
                logf[:, :N_HEADS].reshape(B, T, N_HEADS),
                z[:, Z_KVC:Z_KVC + 2 * HEAD_DIM].reshape(B, T, 2, HEAD_DIM),
                d_lat.reshape(B, T, MLA_KV_RANK),
                d_rope[:, :MLA_ROPE].reshape(B, T, MLA_ROPE))
        if past is None:
            o = _prompt_mixers(z, logf, q_mla, kc, lw, bias_a, bias_c, B, T)
        else:
            o = _paged_mixers(z, rows, q_mla, l, lw, t5_table, past[0], past[1], B, T)
        mixed = _gate_mix(h, lw['w_gates'], o, lw['w_branch'])
        mix = _mm(mixed, lw['w_out'])
        xf, h2 = _norm(xf, T, W['norm_ffn'][l], mix, g1, sc2, sh2)
        delta, gate = _peer(h2, lw), g2
        per_layer.append(rows)
    _, y = _norm(xf, T, W['norm_final'], delta, gate, out_dtype=F32)
    stacked = tuple(jnp.stack([r[i] for r in per_layer], axis=0) for i in range(7))
    return y.reshape(B, T, D), stacked


def kernel(x_prompt, x_sample, cache_a_kv, cache_a_idx, cache_b_kv, cache_b_logf, cache_c_kv,
           cache_d_latent, cache_d_rope, page_table, c_prompt, c_sample, t5_table, w_ada, b_ada,
           norm_mix, norm_ffn, w_in, b_forget, g_cq, g_ckv, w_uq, w_ukv, w_branch, w_out,
           w_peer_q, peer_k1, peer_k2, peer_u, peer_v, norm_final):
    W = {
        'norm_mix': norm_mix, 'norm_ffn': norm_ffn, 'w_in': w_in, 'b_forget': b_forget, 'g_cq': g_cq,
        'g_ckv': g_ckv, 'w_uq': w_uq, 'w_ukv': w_ukv, 'w_branch': w_branch, 'w_out': w_out,
        'w_peer_q': w_peer_q, 'peer_k1': peer_k1, 'peer_k2': peer_k2, 'peer_u': peer_u,
        'peer_v': peer_v, 'norm_final': norm_final,
    }
    caches = {
        'a_kv': cache_a_kv, 'a_idx': cache_a_idx, 'b_kv': cache_b_kv, 'b_logf': cache_b_logf,
        'c_kv': cache_c_kv, 'd_lat': cache_d_latent, 'd_rope': cache_d_rope,
    }
    depth, D = w_in.shape[0], w_in.shape[1]
    n_prompt = c_prompt.shape[0]
    LW = [_layer_weights(l, W, D) for l in range(depth)]
    c_all = jnp.concatenate([c_prompt, c_sample], axis=0)
    mods = [_mm(c_all, w_ada[l], bias=b_ada[l], silu_in=True) for l in range(depth)]
    y_prompt, rows_p = _run_trunk(x_prompt, [m[:n_prompt] for m in mods], None, W, LW, t5_table)
    y_sample, rows_s = _run_trunk(x_sample, [m[n_prompt:] for m in mods], (caches, page_table), W, LW, t5_table)
    return (y_prompt, y_sample) + tuple(rows_p) + tuple(rows_s)
```

```python
import math
from functools import partial

import jax
import jax.numpy as jnp
from jax import lax
from jax.experimental import pallas as pl
from jax.experimental.pallas import tpu as pltpu

PAGE_SIZE = 128
HEAD_DIM = 64
N_HEADS = 8
N_IDX_HEADS = 16
IDX_DIM = 32
DSA_TOPK = 256
MOBA_BLOCK = 256
MOBA_TOPK = 3
MLA_Q_RANK = 384
MLA_KV_RANK = 128
MLA_NOPE = 64
MLA_ROPE = 32
MLA_V = 64
ROPE_THETA = 10000.0
N_BUCKETS = 32
MAX_DISTANCE = 128
N_BRANCH = 4
BRANCH_WIDTH = N_HEADS * HEAD_DIM
PEER_HEADS = 8
PEER_KEY_DIM = 256
PEER_N_KEYS = 128
PEER_TOPK = 16
PEER_SEL = PEER_HEADS * PEER_TOPK
Q_BLOCK = 128
MOBA_Q_BLOCK = 16
EPS = 1e-6
NEG = -1e30

LANES = 128
VMEM_LIMIT = 56 * 1024 * 1024
TQ = 256
MLA_QW = 256

F32 = jnp.float32
BF16 = jnp.bfloat16

Z_CQ, Z_CKV, Z_KR, Z_FB = 0, 384, 512, 640
Z_PREP_W = 768
Z_QA, Z_KVA, Z_QIDX, Z_KIDX = 768, 1280, 1408, 1920
Z_WIDX = Z_KIDX + IDX_DIM
Z_QB, Z_KVB, Z_QC, Z_KVC = 2048, 2560, 2688, 3200
Z_SMALL = 3328


def _in_offsets(d_model):
    layout = (
        ('q_a', N_HEADS * HEAD_DIM), ('k_a', HEAD_DIM), ('v_a', HEAD_DIM),
        ('q_idx', N_IDX_HEADS * IDX_DIM), ('k_idx', IDX_DIM), ('w_idx', N_IDX_HEADS),
        ('q_b', N_HEADS * HEAD_DIM), ('k_b', HEAD_DIM), ('v_b', HEAD_DIM), ('f_b', N_HEADS),
        ('q_c', N_HEADS * HEAD_DIM), ('k_c', HEAD_DIM), ('v_c', HEAD_DIM),
        ('cq_d', MLA_Q_RANK), ('ckv_d', MLA_KV_RANK), ('kr_d', MLA_ROPE),
        ('gates', N_BRANCH * d_model),
    )
    offs, off = {}, 0
    for name, size in layout:
        offs[name] = (off, size)
        off += size
    return offs


def _cparams(*sem):
    return pltpu.CompilerParams(dimension_semantics=sem, vmem_limit_bytes=VMEM_LIMIT)


def _dot(a, b):
    return jnp.dot(a, b, preferred_element_type=F32)


def _dot_nt(a, b):
    return lax.dot_general(a, b, (((1,), (1,)), ((), ())), preferred_element_type=F32)


def _pick(n, cands):
    for c in cands:
        if n % c == 0:
            return c
    return n


def _mm_kernel(*refs, silu_in, has_bias):
    a_ref, w_ref = refs[0], refs[1]
    o_ref = refs[-1]
    a = a_ref[...]
    if silu_in:
        a = a.astype(F32)
        a = a * jax.nn.sigmoid(a)
    acc = _dot(a.astype(BF16), w_ref[...].astype(BF16))
    if has_bias:
        acc = acc + refs[2][...]
    o_ref[...] = acc.astype(o_ref.dtype)


def _mm(a, w, bias=None, silu_in=False, out_dtype=F32):
    M, K = a.shape
    N = w.shape[1]
    tm = _pick(M, (1024, 512, 256, 128))
    tn = _pick(N, (512, 256, 128))
    in_specs = [pl.BlockSpec((tm, K), lambda i, j: (i, 0)),
                pl.BlockSpec((K, tn), lambda i, j: (0, j))]
    args = [a, w]
    if bias is not None:
        in_specs.append(pl.BlockSpec((1, tn), lambda i, j: (0, j)))
        args.append(bias.reshape(1, N).astype(F32))
    return pl.pallas_call(
        partial(_mm_kernel, silu_in=silu_in, has_bias=bias is not None),
        grid=(M // tm, N // tn),
        in_specs=in_specs,
        out_specs=pl.BlockSpec((tm, tn), lambda i, j: (i, j)),
        out_shape=jax.ShapeDtypeStruct((M, N), out_dtype),
        compiler_params=_cparams("parallel", "parallel"),
    )(*args)


def _norm_kernel(*refs, has_delta, has_mod):
    it = iter(refs)
    x_ref = next(it)
    x = x_ref[...]
    if has_delta:
        d_ref, gt_ref = next(it), next(it)
        x = x + gt_ref[...] * d_ref[...]
    g_ref = next(it)
    if has_mod:
        sc_ref, sh_ref = next(it), next(it)
    if has_delta:
        xo_ref = next(it)
        xo_ref[...] = x
    h_ref = next(it)
    y = x * lax.rsqrt(jnp.mean(x * x, axis=-1, keepdims=True) + EPS) * g_ref[...]
    if has_mod:
        y = y * (1.0 + sc_ref[...]) + sh_ref[...]
    h_ref[...] = y.astype(h_ref.dtype)


def _norm(x, T, g, delta=None, gate=None, sc=None, sh=None, out_dtype=BF16):
    n, D = x.shape
    tr = _pick(n, (256, 128)) if T > 1 else n
    if T > 1:
        assert T % tr == 0

    def row_spec():
        return pl.BlockSpec((tr, D), lambda i: (i, 0))

    def mod_arg(a):
        if T == 1:
            return a, row_spec()
        return a[:, None, :], pl.BlockSpec((None, 1, D), lambda i: (i * tr // T, 0, 0))

    has_delta, has_mod = delta is not None, sc is not None
    args, specs = [x], [row_spec()]
    if has_delta:
        ga, gs = mod_arg(gate)
        args += [delta, ga]
        specs += [row_spec(), gs]
    args.append(g.reshape(1, D))
    specs.append(pl.BlockSpec((1, D), lambda i: (0, 0)))
    if has_mod:
        for a in (sc, sh):
            aa, ss = mod_arg(a)
            args.append(aa)
            specs.append(ss)
    out_shape, out_specs = [], []
    if has_delta:
        out_shape.append(jax.ShapeDtypeStruct((n, D), F32))
        out_specs.append(row_spec())
    out_shape.append(jax.ShapeDtypeStruct((n, D), out_dtype))
    out_specs.append(row_spec())
    outs = pl.pallas_call(
        partial(_norm_kernel, has_delta=has_delta, has_mod=has_mod),
        grid=(n // tr,), in_specs=specs, out_specs=out_specs, out_shape=out_shape,
        compiler_params=_cparams("parallel"),
    )(*args)
    if has_delta:
        return outs[0], outs[1]
    return x, outs[0]


def _prep_kernel(z_ref, gcq_ref, gckv_ref, bf_ref, wq_ref, wuk_ref, c128_ref, s128_ref, cq_ref, sq_ref,
                 logf_ref, dlat_ref, drope_ref, qm_ref, kc_ref):
    nope_w = N_HEADS * MLA_NOPE
    qw = N_HEADS * MLA_QW
    cq = z_ref[:, Z_CQ:Z_CQ + MLA_Q_RANK]
    ckv = z_ref[:, Z_CKV:Z_CKV + MLA_KV_RANK]
    kr = z_ref[:, Z_KR:Z_KR + LANES]
    fb = z_ref[:, Z_FB:Z_FB + LANES]

    def rms(x, g):
        return x * lax.rsqrt(jnp.mean(x * x, axis=-1, keepdims=True) + EPS) * g

    cqn = rms(cq, gcq_ref[...]).astype(BF16)
    qall = _dot(cqn, wq_ref[...])
    q_nope = qall[:, :nope_w].astype(BF16)
    q_r = qall[:, nope_w:nope_w + qw]
    q_rs = qall[:, nope_w + qw:]
    q_lat = _dot(q_nope, wuk_ref[...])
    cos_q = jnp.concatenate([cq_ref[...]] * N_HEADS, axis=1)
    sin_q = jnp.concatenate([sq_ref[...]] * N_HEADS, axis=1)
    scale = (MLA_NOPE + MLA_ROPE) ** -0.5
    qm_ref[...] = ((q_lat + q_r * cos_q + q_rs * sin_q) * scale).astype(BF16)

    dlat = rms(ckv, gckv_ref[...])
    dlat_ref[...] = dlat
    kr_sw = pltpu.roll(kr, LANES - MLA_ROPE, axis=1)
    drope = kr * c128_ref[...] + kr_sw * s128_ref[...]
    drope_ref[...] = drope
    kc_ref[...] = jnp.concatenate([dlat, drope], axis=1).astype(BF16)

    xf = fb + bf_ref[...]
    logf_ref[...] = jnp.minimum(xf, 0.0) - jnp.log1p(jnp.exp(-jnp.abs(xf)))


def _prep(z, T, lw, rope):
    n = z.shape[0]
    tm = _pick(n, (512, 256, 128))
    c128, s128, cq, sq = rope
    if T == 1:
        def tab_spec(w):
            return pl.BlockSpec((1, w), lambda i: (0, 0))
    else:
        assert T % tm == 0
        nt = T // tm

        def tab_spec(w):
            return pl.BlockSpec((tm, w), lambda i: (i % nt, 0))

    def full(a):
        return pl.BlockSpec(a.shape, lambda i: (0,) * a.ndim)

    def rows(w):
        return pl.BlockSpec((tm, w), lambda i: (i, 0))

    return pl.pallas_call(
        _prep_kernel,
        grid=(n // tm,),
        in_specs=[pl.BlockSpec((tm, Z_PREP_W), lambda i: (i, 0)),
                  full(lw['g_cq']), full(lw['g_ckv']), full(lw['b_forget']), full(lw['wq_all']),
                  full(lw['wuk_bd']), tab_spec(LANES), tab_spec(LANES), tab_spec(MLA_QW), tab_spec(MLA_QW)],
        out_specs=[rows(LANES), rows(LANES), rows(LANES), rows(N_HEADS * MLA_QW), rows(MLA_QW)],
        out_shape=[jax.ShapeDtypeStruct((n, LANES), F32), jax.ShapeDtypeStruct((n, LANES), F32),
                   jax.ShapeDtypeStruct((n, LANES), F32),
                   jax.ShapeDtypeStruct((n, N_HEADS * MLA_QW), BF16),
                   jax.ShapeDtypeStruct((n, MLA_QW), BF16)],
        compiler_params=_cparams("parallel"),
    )(z, lw['g_cq'], lw['g_ckv'], lw['b_forget'], lw['wq_all'], lw['wuk_bd'], c128, s128, cq, sq)


CUM_CHUNK = 256


def _cumsum_kernel(x_ref, o_ref):
    T = x_ref.shape[0]
    r = lax.broadcasted_iota(jnp.int32, (CUM_CHUNK, CUM_CHUNK), 0)
    c = lax.broadcasted_iota(jnp.int32, (CUM_CHUNK, CUM_CHUNK), 1)
    tri = jnp.where(r >= c, 1.0, 0.0).astype(BF16)

    def body(i, carry):
        off = pl.multiple_of(i * CUM_CHUNK, CUM_CHUNK)
        x = x_ref[pl.ds(off, CUM_CHUNK), :]
        hi = x.astype(BF16)
        r1 = x - hi.astype(F32)
        mid = r1.astype(BF16)
        lo = (r1 - mid.astype(F32)).astype(BF16)
        f = _dot(tri, hi) + _dot(tri, mid) + _dot(tri, lo) + carry
        o_ref[pl.ds(off, CUM_CHUNK), :] = f
        return f[CUM_CHUNK - 1:CUM_CHUNK, :]

    lax.fori_loop(0, T // CUM_CHUNK, body, jnp.zeros((1, LANES), F32))


def _cumsum(x):
    B, T, _ = x.shape
    assert T % CUM_CHUNK == 0
    spec = pl.BlockSpec((None, T, LANES), lambda b: (b, 0, 0))
    return pl.pallas_call(
        _cumsum_kernel, grid=(B,), in_specs=[spec], out_specs=spec,
        out_shape=jax.ShapeDtypeStruct(x.shape, F32), compiler_params=_cparams("parallel"),
    )(x)


def _softmax_pv(s, v):
    m = jnp.max(s, axis=-1, keepdims=True)
    p = jnp.exp(s - m)
    l = jnp.sum(p, axis=-1, keepdims=True)
    return _dot(p.astype(BF16), v) / l


def _causal(qi, T):
    row = lax.broadcasted_iota(jnp.int32, (TQ, T), 0) + qi * TQ
    col = lax.broadcasted_iota(jnp.int32, (TQ, T), 1)
    return row, col


def _count(mask):
    return jnp.sum(jnp.where(mask, 1.0, 0.0), axis=-1, keepdims=True)


def _fox_kernel(q_ref, k_ref, v_ref, fq_ref, fk_ref, o_ref):
    qi = pl.program_id(1)
    k, v = k_ref[...], v_ref[...]
    row, col = _causal(qi, k.shape[0])
    causal = col <= row

    def body(h, _):
        s = _dot_nt(q_ref[h], k) + fq_ref[h] - fk_ref[h]
        o_ref[h] = _softmax_pv(jnp.where(causal, s, NEG), v)
        return 0

    lax.fori_loop(0, N_HEADS, body, 0)


def _mla_kernel(q_ref, kc_ref, wuv_ref, o_ref):
    qi = pl.program_id(1)
    kc = kc_ref[...]
    v = kc[:, :MLA_KV_RANK]
    row, col = _causal(qi, kc.shape[0])
    causal = col <= row

    def body(h, _):
        s = _dot_nt(q_ref[h], kc)
        o_lat = _softmax_pv(jnp.where(causal, s, NEG), v)
        o_ref[h] = _dot(o_lat.astype(BF16), wuv_ref[h])
        return 0

    lax.fori_loop(0, N_HEADS, body, 0)


def _biased_logits(q, k, h, qi, dt_ref, c_ref, s_ref):
    off = pl.multiple_of(qi * TQ, TQ)
    s_ref[...] = _dot_nt(q, k) + c_ref[h]
    s_ref[:, pl.ds(off, TQ)] += dt_ref[0, h]

    @pl.when(qi > 0)
    def _():
        s_ref[:, pl.ds(off - TQ, TQ)] += dt_ref[1, h]

    return s_ref[...]


def _moba_kernel(q_ref, k_ref, v_ref, kf_ref, dt_ref, c_ref, o_ref, s_ref, *, k_sel):
    qi = pl.program_id(1)
    k, v = k_ref[...], v_ref[...]
    T = k.shape[0]
    nb = T // MOBA_BLOCK
    km = jnp.mean(kf_ref[...].reshape(nb, MOBA_BLOCK, HEAD_DIM), axis=1)
    km = jnp.concatenate([km, jnp.zeros((LANES - nb, HEAD_DIM), F32)], axis=0).astype(BF16)
    row, col = _causal(qi, T)
    own = (col >= qi * TQ) & (col <= row)
    lane = lax.broadcasted_iota(jnp.int32, (TQ, LANES), 1)
    eb = lax.broadcasted_iota(jnp.int32, (LANES, T), 0)
    es = lax.broadcasted_iota(jnp.int32, (LANES, T), 1)
    expand = jnp.where(es // MOBA_BLOCK == eb, 1.0, 0.0).astype(BF16)

    def body(h, _):
        q = q_ref[h]
        gate = _dot_nt(q, km)
        rank = jnp.zeros((TQ, LANES), F32)
        for m in range(nb):
            gm = gate[:, m:m + 1]
            beats = (gm > gate) | ((gm == gate) & (lane > m))
            rank = rank + jnp.where(beats, jnp.where(m < qi, 1.0, 0.0), 0.0)
        selb = jnp.where((rank < k_sel) & (lane < qi), 1.0, 0.0).astype(BF16)
        picked = _dot(selb, expand) > 0.5
        s = _biased_logits(q, k, h, qi, dt_ref, c_ref, s_ref)
        o_ref[h] = _softmax_pv(jnp.where(picked | own, s, NEG), v)
        return 0

    lax.fori_loop(0, N_HEADS, body, 0)


def _sortable(x):
    b = pltpu.bitcast(x, jnp.int32)
    return b ^ ((b >> 31) & jnp.int32(0x7FFFFFFF))


def _topk_mask(score, col, k_sel, n_idx_bits, tail=None):
    key = _sortable(score)
    tkey = None if tail is None else _sortable(tail)
    kf = float(k_sel)

    def count_ge(c):
        n = _count(key >= c)
        return n if tkey is None else n + jnp.where(tkey >= c, 1.0, 0.0)

    prefix = jnp.where(count_ge(jnp.int32(0)) >= kf, jnp.int32(0), jnp.int32(-2 ** 31))

    def vbody(i, prefix):
        cand = prefix | (jnp.int32(1) << (30 - i))
        return jnp.where(count_ge(cand) >= kf, cand, prefix)

    thr = lax.fori_loop(0, 31, vbody, prefix)
    above = key > thr
    tie = key == thr
    n_above = _count(above)
    if tkey is not None:
        n_above = n_above + jnp.where(tkey > thr, 1.0, 0.0)
    need = kf - n_above

    def ibody(i, p):
        cand = p | (jnp.int32(1) << (n_idx_bits - 1 - i))
        return jnp.where(_count(tie & (col < cand)) < need, cand, p)

    last = lax.fori_loop(0, n_idx_bits, ibody, jnp.zeros_like(thr))
    mask = above | (tie & (col <= last))
    if tkey is None:
        return mask
    return mask, (tkey > thr) | ((tkey == thr) & (_count(tie) < need))


def _dsa_kernel(q_ref, qi_ref, w_ref, kidx_ref, k_ref, v_ref, dt_ref, c_ref, o_ref, s_ref, sc_ref, *, k_sel):
    qi = pl.program_id(1)
    k, v, kidx = k_ref[...], v_ref[...], kidx_ref[...]
    T = k.shape[0]
    row, col = _causal(qi, T)
    causal = col <= row

    sc_ref[...] = jnp.zeros_like(sc_ref)

    def ibody(h, _):
        sc_ref[...] += w_ref[h] * jnp.maximum(_dot_nt(qi_ref[h], kidx), 0.0)
        return 0

    lax.fori_loop(0, N_IDX_HEADS, ibody, 0)
    score = jnp.where(causal, sc_ref[...], NEG)
    mask = _topk_mask(score, col, k_sel, max(1, (T - 1).bit_length())) & causal

    def body(h, _):
        s = _biased_logits(q_ref[h], k, h, qi, dt_ref, c_ref, s_ref)
        o_ref[h] = _softmax_pv(jnp.where(mask, s, NEG), v)
        return 0

    lax.fori_loop(0, N_HEADS, body, 0)


def _heads_spec(nh, w):
    return pl.BlockSpec((None, nh, TQ, w), lambda b, i: (b, 0, i, 0))


def _ctx_spec(T, w):
    return pl.BlockSpec((None, T, w), lambda b, i: (b, 0, 0))


def _full_spec(a):
    return pl.BlockSpec(a.shape, lambda b, i: (0,) * a.ndim)


def _attn_call(kern, B, T, in_specs, args, scratch=()):
    return pl.pallas_call(
        kern, grid=(B, T // TQ), in_specs=in_specs,
        out_specs=_heads_spec(N_HEADS, HEAD_DIM),
        out_shape=jax.ShapeDtypeStruct((B, N_HEADS, T, HEAD_DIM), F32),
        scratch_shapes=list(scratch),
        compiler_params=_cparams("parallel", "arbitrary"),
    )(*args)


def _to_heads(x, B, T, nh, d, scale=1.0):
    return (x.reshape(B, T, nh, d) * scale).astype(BF16).transpose(0, 2, 1, 3)


def _from_heads(o):
    B, nh, T, d = o.shape
    return o.transpose(0, 2, 1, 3).reshape(B * T, nh * d).astype(BF16)


def _prompt_mixers(z, logf, q_mla, kc, lw, bias_a, bias_c, B, T):
    assert T % TQ == 0 and TQ == MOBA_BLOCK and MAX_DISTANCE <= TQ
    scale = HEAD_DIM ** -0.5

    def ctx(lo, w):
        return z[:, lo:lo + w].reshape(B, T, w)

    sblock = pltpu.VMEM((TQ, T), F32)
    q_a = _to_heads(z[:, Z_QA:Z_QA + BRANCH_WIDTH], B, T, N_HEADS, HEAD_DIM, scale)
    q_idx = _to_heads(z[:, Z_QIDX:Z_QIDX + N_IDX_HEADS * IDX_DIM], B, T, N_IDX_HEADS, IDX_DIM)
    w_idx = (z[:, Z_WIDX:Z_WIDX + N_IDX_HEADS] * N_IDX_HEADS ** -0.5).reshape(B, T, N_IDX_HEADS)
    w_idx = w_idx.transpose(0, 2, 1)[..., None]
    k_idx = ctx(Z_KIDX, IDX_DIM).astype(BF16)
    ka = ctx(Z_KVA, HEAD_DIM).astype(BF16)
    va = ctx(Z_KVA + HEAD_DIM, HEAD_DIM).astype(BF16)
    dt_a, c_a = bias_a
    o_a = _attn_call(
        partial(_dsa_kernel, k_sel=min(DSA_TOPK, T // 4)), B, T,
        [_heads_spec(N_HEADS, HEAD_DIM), _heads_spec(N_IDX_HEADS, IDX_DIM), _heads_spec(N_IDX_HEADS, 1),
         _ctx_spec(T, IDX_DIM), _ctx_spec(T, HEAD_DIM), _ctx_spec(T, HEAD_DIM), _full_spec(dt_a), _full_spec(c_a)],
        (q_a, q_idx, w_idx, k_idx, ka, va, dt_a, c_a), scratch=(sblock, sblock))
    q_b = _to_heads(z[:, Z_QB:Z_QB + BRANCH_WIDTH], B, T, N_HEADS, HEAD_DIM, scale)
    kb = ctx(Z_KVB, HEAD_DIM).astype(BF16)
    vb = ctx(Z_KVB + HEAD_DIM, HEAD_DIM).astype(BF16)
    f_cum = _cumsum(logf.reshape(B, T, LANES))[:, :, :N_HEADS].transpose(0, 2, 1)
    o_b = _attn_call(
        _fox_kernel, B, T,
        [_heads_spec(N_HEADS, HEAD_DIM), _ctx_spec(T, HEAD_DIM), _ctx_spec(T, HEAD_DIM), _heads_spec(N_HEADS, 1),
         pl.BlockSpec((None, N_HEADS, 1, T), lambda b, i: (b, 0, 0, 0))],
        (q_b, kb, vb, f_cum[..., None], f_cum[:, :, None, :]))
    q_c = _to_heads(z[:, Z_QC:Z_QC + BRANCH_WIDTH], B, T, N_HEADS, HEAD_DIM, scale)
    kc_f = ctx(Z_KVC, HEAD_DIM)
    vc = ctx(Z_KVC + HEAD_DIM, HEAD_DIM).astype(BF16)
    dt_c, c_c = bias_c
    o_c = _attn_call(
        partial(_moba_kernel, k_sel=min(MOBA_TOPK, T // MOBA_BLOCK)), B, T,
        [_heads_spec(N_HEADS, HEAD_DIM), _ctx_spec(T, HEAD_DIM), _ctx_spec(T, HEAD_DIM), _ctx_spec(T, HEAD_DIM),
         _full_spec(dt_c), _full_spec(c_c)],
        (q_c, kc_f.astype(BF16), vc, kc_f, dt_c, c_c), scratch=(sblock,))
    q_d = q_mla.reshape(B, T, N_HEADS, MLA_QW).transpose(0, 2, 1, 3)
    o_d = _attn_call(
        _mla_kernel, B, T,
        [_heads_spec(N_HEADS, MLA_QW), _ctx_spec(T, MLA_QW), _full_spec(lw['w_uv'])],
        (q_d, kc.reshape(B, T, MLA_QW), lw['w_uv']))
    return jnp.stack([_from_heads(o) for o in (o_a, o_b, o_c, o_d)], axis=0)


def _gate_mix_kernel(h_ref, g0_ref, g1_ref, g2_ref, g3_ref, o_ref, wb_ref, out_ref):
    h = h_ref[...]
    acc = None
    for n, g_ref in enumerate((g0_ref, g1_ref, g2_ref, g3_ref)):
        gate = jax.nn.sigmoid(_dot(h, g_ref[...]))
        term = gate * _dot(o_ref[n], wb_ref[n])
        acc = term if acc is None else acc + term
    out_ref[...] = acc.astype(out_ref.dtype)


def _gate_mix(h, w_gates, o, w_branch):
    n, D = h.shape
    tm = _pick(n, (1024, 512, 256, 128))
    td = _pick(D, (256, 128))
    nd = D // td
    gate_specs = [pl.BlockSpec((D, td), partial(lambda i, j, b: (0, b * nd + j), b=b)) for b in range(N_BRANCH)]
    return pl.pallas_call(
        _gate_mix_kernel,
        grid=(n // tm, nd),
        in_specs=[pl.BlockSpec((tm, D), lambda i, j: (i, 0))] + gate_specs + [
            pl.BlockSpec((N_BRANCH, tm, BRANCH_WIDTH), lambda i, j: (0, i, 0)),
            pl.BlockSpec((N_BRANCH, BRANCH_WIDTH, td), lambda i, j: (0, 0, j))],
        out_specs=pl.BlockSpec((tm, td), lambda i, j: (i, j)),
        out_shape=jax.ShapeDtypeStruct((n, D), BF16),
        compiler_params=_cparams("parallel", "parallel"),
    )(h, w_gates, w_gates, w_gates, w_gates, o, w_branch)


def _top16(s):
    R, tn = s.shape
    rid = lax.broadcasted_iota(jnp.int32, (R, tn), 0)
    kid = lax.broadcasted_iota(jnp.int32, (PEER_TOPK, tn), 0)

    def body(k, carry):
        s, vals, idxs = carry
        m = jnp.max(s, axis=0, keepdims=True)
        i = jnp.min(jnp.where(s == m, rid, R), axis=0, keepdims=True)
        vals = jnp.where(kid == k, m, vals)
        idxs = jnp.where(kid == k, i, idxs)
        return jnp.where(rid == i, -jnp.inf, s), vals, idxs

    _, vals, idxs = lax.fori_loop(
        0, PEER_TOPK, body, (s, jnp.zeros((PEER_TOPK, tn), F32), jnp.zeros((PEER_TOPK, tn), jnp.int32)))
    return vals, idxs


def _peer_select_kernel(q_ref, k1_ref, k2_ref, e1_ref, e2_ref, g_ref):
    half = PEER_KEY_DIM // 2
    k1, k2 = k1_ref[...], k2_ref[...]

    def pick(idx, table):
        out = jnp.zeros_like(idx)
        for j in range(PEER_TOPK):
            out = out + jnp.where(idx == j, table[j:j + 1, :], 0)
        return out

    def body(h, _):
        off = pl.multiple_of(h * PEER_KEY_DIM, PEER_KEY_DIM)
        q1 = q_ref[:, pl.ds(off, half)].astype(BF16)
        q2 = q_ref[:, pl.ds(off + half, half)].astype(BF16)
        v1, i1 = _top16(_dot_nt(k1, q1))
        v2, i2 = _top16(_dot_nt(k2, q2))
        cand = jnp.concatenate([v1[a:a + 1, :] + v2 for a in range(PEER_TOPK)], axis=0)
        top, ci = _top16(cand)
        e = jnp.exp(top - jnp.max(top, axis=0, keepdims=True))
        rows = pl.ds(pl.multiple_of(h * PEER_TOPK, PEER_TOPK), PEER_TOPK)
        e1_ref[rows, :] = pick(ci // PEER_TOPK, i1)
        e2_ref[rows, :] = pick(ci % PEER_TOPK, i2)
        g_ref[rows, :] = e / jnp.sum(e, axis=0, keepdims=True)
        return 0

    lax.fori_loop(0, PEER_HEADS, body, 0)


def _peer_select(q, k1, k2):
    n = q.shape[0]
    tn = _pick(n, (256, 128))
    out_spec = pl.BlockSpec((PEER_SEL, tn), lambda i: (0, i))
    kspec = pl.BlockSpec(k1.shape, lambda i: (0, 0))
    return pl.pallas_call(
        _peer_select_kernel, grid=(n // tn,),
        in_specs=[pl.BlockSpec((tn, PEER_HEADS * PEER_KEY_DIM), lambda i: (i, 0)), kspec, kspec],
        out_specs=[out_spec] * 3,
        out_shape=[jax.ShapeDtypeStruct((PEER_SEL, n), jnp.int32)] * 2 + [jax.ShapeDtypeStruct((PEER_SEL, n), F32)],
        compiler_params=_cparams("parallel"),
    )(q, k1, k2)


PEER_G_TOKENS = 32


def _peer_gates_kernel(e1_ref, e2_ref, g_ref, o_ref):
    shape = (PEER_G_TOKENS, PEER_N_KEYS, PEER_SEL)
    rid = lax.broadcasted_iota(jnp.int32, shape, 1)
    p1 = jnp.where(e1_ref[...][:, None, :] == rid, 1.0, 0.0).astype(BF16)
    gp2 = jnp.where(e2_ref[...][:, None, :] == rid, g_ref[...][:, None, :], 0.0).astype(BF16)
    o_ref[...] = jnp.einsum('nrk,nck->nrc', p1, gp2, preferred_element_type=F32).astype(o_ref.dtype)


def _peer_gates(e1, e2, g):
    n = e1.shape[0]
    spec = pl.BlockSpec((PEER_G_TOKENS, PEER_SEL), lambda i: (i, 0))
    return pl.pallas_call(
        _peer_gates_kernel, grid=(n // PEER_G_TOKENS,),
        in_specs=[spec, spec, spec],
        out_specs=pl.BlockSpec((PEER_G_TOKENS, PEER_N_KEYS, PEER_N_KEYS), lambda i: (i, 0, 0)),
        out_shape=jax.ShapeDtypeStruct((n, PEER_N_KEYS, PEER_N_KEYS), BF16),
        compiler_params=_cparams("parallel"),
    )(e1, e2, g)


def _peer_dense_kernel(h_ref, u_ref, g_ref, v_ref, o_ref):
    @pl.when(pl.program_id(1) == 0)
    def _():
        o_ref[...] = jnp.zeros_like(o_ref)

    a = _dot_nt(h_ref[...], u_ref[...])
    act = 0.5 * a * (1.0 + lax.erf(a * (0.5 ** 0.5)))
    w = (g_ref[...].astype(F32) * act).astype(BF16)
    o_ref[...] += _dot(w, v_ref[...])


def _peer_dense(h, u, g, v):
    n, D = h.shape
    E = u.shape[0]
    tn = _pick(n, (512, 256, 128))
    te = _pick(E, (512, 256, 128))
    return pl.pallas_call(
        _peer_dense_kernel, grid=(n // tn, E // te),
        in_specs=[pl.BlockSpec((tn, D), lambda i, j: (i, 0)), pl.BlockSpec((te, D), lambda i, j: (j, 0)),
                  pl.BlockSpec((tn, te), lambda i, j: (i, j)), pl.BlockSpec((te, D), lambda i, j: (j, 0))],
        out_specs=pl.BlockSpec((tn, D), lambda i, j: (i, 0)),
        out_shape=jax.ShapeDtypeStruct((n, D), F32),
        compiler_params=_cparams("parallel", "arbitrary"),
    )(h, u, g, v)


def _peer(h, lw):
    q = _mm(h, lw['w_peer_q'])
    e1, e2, g = _peer_select(q, lw['peer_k1'], lw['peer_k2'])
    grid = _peer_gates(e1.T, e2.T, g.T)
    return _peer_dense(h, lw['peer_u'], grid.reshape(h.shape[0], PEER_N_KEYS * PEER_N_KEYS), lw['peer_v'])


def _rel_bucket(rel):
    n_exact = N_BUCKETS // 2
    relf = jnp.maximum(rel, 1).astype(F32)
    large = n_exact + (jnp.log(relf / n_exact) / math.log(MAX_DISTANCE / n_exact)
                       * (N_BUCKETS - n_exact)).astype(jnp.int32)
    large = jnp.minimum(large, N_BUCKETS - 1)
    return jnp.where(rel < n_exact, jnp.maximum(rel, 0), large)


def _dec_softmax(s, s_new, vmat, v_new):
    m = jnp.maximum(jnp.max(s, axis=-1, keepdims=True), s_new)
    p = jnp.exp(s - m)
    pn = jnp.exp(s_new - m)
    l = jnp.sum(p, axis=-1, keepdims=True) + pn
    return (_dot(p.astype(BF16), vmat) + pn * v_new) / l


def _new_logit(q, new_row):
    return jnp.sum(q.astype(F32) * new_row.astype(BF16).astype(F32), axis=-1, keepdims=True)


def _fox_dec_kernel(q_ref, kv_ref, new_ref, fk_ref, ln_ref, o_ref):
    q, new = q_ref[...].astype(BF16), new_ref[...]
    kvb = kv_ref[...].astype(BF16)
    fk = fk_ref[...]
    f_t = fk[:, fk.shape[1] - 1:] + ln_ref[...]
    s = _dot_nt(q, kvb) + (f_t - fk)
    o = _dec_softmax(s, _new_logit(q, new), kvb, new)
    o_ref[...] = o[:, HEAD_DIM:]


def _dsa_dec_kernel(q_ref, qi_ref, w_ref, kidx_ref, kinew_ref, kv_ref, new_ref, bias_ref, bnew_ref, o_ref, *, k_sel):
    q, new, w = q_ref[...].astype(BF16), new_ref[...], w_ref[...]
    qidx = qi_ref[...].astype(BF16)
    dots = jnp.maximum(_dot_nt(qidx, kidx_ref[...].astype(BF16)), 0.0)
    score = jnp.sum(w * dots, axis=0, keepdims=True)
    score_new = jnp.sum(w * jnp.maximum(_new_logit(qidx, kinew_ref[...]), 0.0), axis=0, keepdims=True)
    Lp = score.shape[1]
    col = lax.broadcasted_iota(jnp.int32, (1, Lp), 1)
    mask, mask_new = _topk_mask(score, col, k_sel, max(1, (Lp - 1).bit_length()), tail=score_new)
    kvb = kv_ref[...].astype(BF16)
    s = jnp.where(mask, _dot_nt(q, kvb) + bias_ref[...], NEG)
    s_new = jnp.where(mask_new, _new_logit(q, new) + bnew_ref[...], NEG)
    o_ref[...] = _dec_softmax(s, s_new, kvb, new)[:, HEAD_DIM:]


def _moba_dec_kernel(q_ref, kv_ref, new_ref, exp_ref, bias_ref, bnew_ref, o_ref, *, k_sel):
    q, new = q_ref[...].astype(BF16), new_ref[...]
    kv = kv_ref[...]
    Lp = kv.shape[0]
    nb = Lp // MOBA_BLOCK
    km = jnp.mean(kv.reshape(nb, MOBA_BLOCK, 2 * HEAD_DIM), axis=1)
    km = jnp.concatenate([km, jnp.zeros((LANES - nb, 2 * HEAD_DIM), F32)], axis=0).astype(BF16)
    gate = _dot_nt(q, km)
    lane = lax.broadcasted_iota(jnp.int32, gate.shape, 1)
    rank = jnp.zeros(gate.shape, F32)
    for m in range(nb):
        gm = gate[:, m:m + 1]
        rank = rank + jnp.where((gm > gate) | ((gm == gate) & (lane > m)), 1.0, 0.0)
    selb = jnp.where((rank < k_sel) & (lane < nb), 1.0, 0.0).astype(BF16)
    picked = _dot(selb, exp_ref[...]) > 0.5
    kvb = kv.astype(BF16)
    s = jnp.where(picked, _dot_nt(q, kvb) + bias_ref[...], NEG)
    o_ref[...] = _dec_softmax(s, _new_logit(q, new) + bnew_ref[...], kvb, new)[:, HEAD_DIM:]


def _mla_dec_kernel(ql_ref, qr_ref, lat_ref, rope_ref, lnew_ref, rnew_ref, o_ref):
    ql, qr = ql_ref[...].astype(BF16), qr_ref[...].astype(BF16)
    latb = lat_ref[...].astype(BF16)
    s = _dot_nt(ql, latb) + _dot_nt(qr, rope_ref[...].astype(BF16))
    s_new = _new_logit(ql, lnew_ref[...]) + _new_logit(qr, rnew_ref[...])
    o_ref[...] = _dec_softmax(s, s_new, latb, lnew_ref[...])


def _head_proj_kernel(x_ref, w_ref, o_ref):
    o_ref[...] = _dot(x_ref[...].astype(BF16), w_ref[...])


def _head_proj(x, w):
    H, B, c = x.shape
    n = w.shape[2]
    return pl.pallas_call(
        _head_proj_kernel, grid=(H,),
        in_specs=[pl.BlockSpec((None, B, c), lambda h: (h, 0, 0)), pl.BlockSpec((None, c, n), lambda h: (h, 0, 0))],
        out_specs=pl.BlockSpec((None, B, n), lambda h: (h, 0, 0)),
        out_shape=jax.ShapeDtypeStruct((H, B, n), F32),
        compiler_params=_cparams("parallel"),
    )(x, w)


def _dec_call(kern, B, in_specs, args, width=HEAD_DIM):
    return pl.pallas_call(
        kern, grid=(B,), in_specs=in_specs,
        out_specs=pl.BlockSpec((None, N_HEADS, width), lambda b: (b, 0, 0)),
        out_shape=jax.ShapeDtypeStruct((B, N_HEADS, width), F32),
        compiler_params=_cparams("parallel"),
    )(*args)


def _per_seq(a):
    return pl.BlockSpec((None,) + a.shape[1:], lambda b: (b,) + (0,) * (a.ndim - 1))


def _shared(a):
    return pl.BlockSpec(a.shape, lambda b: (0,) * a.ndim)


def _paged_mixers(z, rows, logf, q_mla, l, lw, t5_table, caches, page_table, B):
    n_pages = page_table.shape[1]
    Lp = n_pages * PAGE_SIZE
    assert Lp % MOBA_BLOCK == 0 and Lp // MOBA_BLOCK <= LANES and MAX_DISTANCE <= Lp
    scale = HEAD_DIM ** -0.5

    def past(name, w):
        return caches[name][l][page_table].reshape(B, Lp, w)

    def q_pad(lo):
        q = z[:, lo:lo + BRANCH_WIDTH].reshape(B, N_HEADS, HEAD_DIM) * scale
        return jnp.pad(q, ((0, 0), (0, 0), (0, HEAD_DIM)))

    def new_row(lo, w):
        return z[:, lo:lo + w].reshape(B, 1, w)

    def bias_rows(t5):
        rel = Lp - jnp.arange(Lp, dtype=jnp.int32)
        return t5[_rel_bucket(rel)].T, t5[0].reshape(-1, 1)

    kv_w = 2 * HEAD_DIM
    q_a = q_pad(Z_QA)
    q_idx = z[:, Z_QIDX:Z_QIDX + N_IDX_HEADS * IDX_DIM].reshape(B, N_IDX_HEADS, IDX_DIM)
    w_idx = (z[:, Z_WIDX:Z_WIDX + N_IDX_HEADS] * N_IDX_HEADS ** -0.5).reshape(B, N_IDX_HEADS, 1)
    args = (q_a, q_idx, w_idx, past('a_idx', IDX_DIM), new_row(Z_KIDX, IDX_DIM), past('a_kv', kv_w),
            new_row(Z_KVA, kv_w)) + bias_rows(t5_table[:, :N_HEADS])
    o_a = _dec_call(partial(_dsa_dec_kernel, k_sel=min(DSA_TOPK, (Lp + 1) // 4)), B,
                    [_per_seq(a) for a in args[:7]] + [_shared(a) for a in args[7:]], args)
    lf = past('b_logf', N_HEADS).transpose(1, 0, 2).reshape(Lp, B * N_HEADS // LANES, LANES).transpose(1, 0, 2)
    f_past = _cumsum(lf).transpose(1, 0, 2).reshape(Lp, B, N_HEADS).transpose(1, 2, 0)
    args = (q_pad(Z_QB), past('b_kv', kv_w), new_row(Z_KVB, kv_w), f_past, logf[:, :N_HEADS].reshape(B, N_HEADS, 1))
    o_b = _dec_call(_fox_dec_kernel, B, [_per_seq(a) for a in args], args)
    blk = jnp.arange(LANES, dtype=jnp.int32)[:, None]
    expand = (jnp.arange(Lp, dtype=jnp.int32)[None, :] // MOBA_BLOCK == blk).astype(BF16)
    args = (q_pad(Z_QC), past('c_kv', kv_w), new_row(Z_KVC, kv_w), expand) + bias_rows(t5_table[:, N_HEADS:])
    o_c = _dec_call(partial(_moba_dec_kernel, k_sel=min(MOBA_TOPK, Lp // MOBA_BLOCK + 1)), B,
                    [_per_seq(a) for a in args[:3]] + [_shared(a) for a in args[3:]], args)
    qm = q_mla.reshape(B, N_HEADS, MLA_QW).astype(F32)
    d_lat, d_rope = rows[5], rows[6]
    args = (qm[..., :MLA_KV_RANK], qm[..., MLA_KV_RANK:MLA_KV_RANK + MLA_ROPE], past('d_lat', MLA_KV_RANK),
            past('d_rope', MLA_ROPE), d_lat.reshape(B, 1, MLA_KV_RANK), d_rope.reshape(B, 1, MLA_ROPE))
    o_lat = _dec_call(_mla_dec_kernel, B, [_per_seq(a) for a in args], args, width=MLA_KV_RANK)
    o_d = _head_proj(o_lat.transpose(1, 0, 2), lw['w_uv']).transpose(1, 0, 2)
    return jnp.stack([o.reshape(B, BRANCH_WIDTH).astype(BF16) for o in (o_a, o_b, o_c, o_d)], axis=0)


def _layer_weights(l, W, D):
    offs = _in_offsets(D)
    w_in = W['w_in'][l]

    def seg(name):
        lo, size = offs[name]
        return w_in[:, lo:lo + size]

    def zeros(k):
        return jnp.zeros((D, k), F32)

    half = MLA_ROPE // 2
    kr = seg('kr_d')
    kr_sw = jnp.concatenate([kr[:, half:], kr[:, :half]], axis=1)
    w_small = jnp.concatenate([
        seg('cq_d'), seg('ckv_d'), kr, kr_sw, zeros(LANES - 2 * MLA_ROPE), seg('f_b'), zeros(LANES - N_HEADS),
        seg('q_a'), seg('k_a'), seg('v_a'), seg('q_idx'), seg('k_idx'), seg('w_idx'),
        zeros(LANES - IDX_DIM - N_IDX_HEADS), seg('q_b'), seg('k_b'), seg('v_b'), seg('q_c'), seg('k_c'), seg('v_c'),
    ], axis=1).astype(BF16)
    assert w_small.shape[1] == Z_SMALL

    dq = MLA_NOPE + MLA_ROPE
    w_uq = W['w_uq'][l].reshape(MLA_Q_RANK, N_HEADS, dq)
    w_nope = w_uq[:, :, :MLA_NOPE].reshape(MLA_Q_RANK, N_HEADS * MLA_NOPE)
    rope_w = w_uq[:, :, MLA_NOPE:]
    rope_sw = jnp.concatenate([rope_w[..., half:], rope_w[..., :half]], axis=-1)

    def widen(r):
        wide = jnp.zeros((MLA_Q_RANK, N_HEADS, MLA_QW), F32).at[:, :, MLA_KV_RANK:MLA_KV_RANK + MLA_ROPE].set(r)
        return wide.reshape(MLA_Q_RANK, N_HEADS * MLA_QW)

    wq_all = jnp.concatenate([w_nope, widen(rope_w), widen(rope_sw)], axis=1).astype(BF16)
    w_ukv = W['w_ukv'][l]
    w_uk = jnp.transpose(w_ukv[..., :MLA_NOPE], (1, 2, 0))
    w_uk = jnp.pad(w_uk, ((0, 0), (0, 0), (0, MLA_QW - MLA_KV_RANK)))
    wuk_bd = jnp.einsum('hnc,hg->hngc', w_uk, jnp.eye(N_HEADS, dtype=F32))
    wuk_bd = wuk_bd.reshape(N_HEADS * MLA_NOPE, N_HEADS * MLA_QW).astype(BF16)
    w_uv = jnp.transpose(w_ukv[..., MLA_NOPE:], (1, 0, 2)).astype(BF16)

    return {
        'w_small': w_small, 'w_gates': seg('gates').astype(BF16),
        'wq_all': wq_all, 'wuk_bd': wuk_bd, 'w_uv': w_uv,
        'g_cq': W['g_cq'][l].reshape(1, -1), 'g_ckv': W['g_ckv'][l].reshape(1, -1),
        'b_forget': jnp.pad(W['b_forget'][l], (0, LANES - N_HEADS)).reshape(1, LANES),
        'w_branch': W['w_branch'][l].astype(BF16), 'w_out': W['w_out'][l].astype(BF16),
        'w_peer_q': W['w_peer_q'][l].astype(BF16),
        'peer_k1': W['peer_k1'][l].astype(BF16), 'peer_k2': W['peer_k2'][l].astype(BF16),
        'peer_u': W['peer_u'][l].astype(BF16), 'peer_v': W['peer_v'][l].astype(BF16),
    }


def _rope_tables(pos):
    half = MLA_ROPE // 2
    inv_freq = ROPE_THETA ** (-jnp.arange(half, dtype=F32) / half)
    ang = pos[:, None].astype(F32) * inv_freq
    cos, sin = jnp.cos(ang), jnp.sin(ang)
    pad = jnp.zeros((pos.shape[0], LANES - MLA_ROPE), F32)
    c128 = jnp.concatenate([cos, cos, pad], axis=1)
    s128 = jnp.concatenate([-sin, sin, pad], axis=1)
    lead = jnp.zeros((pos.shape[0], MLA_KV_RANK), F32)
    return c128, s128, jnp.concatenate([lead, c128], axis=1), jnp.concatenate([lead, s128], axis=1)


def _bias_tiles(t5):
    i = jnp.arange(TQ, dtype=jnp.int32)
    rel = jnp.arange(2, dtype=jnp.int32)[:, None, None] * TQ + i[None, :, None] - i[None, None, :]
    far = t5[N_BUCKETS - 1]
    tiles = jnp.where((rel >= 0)[..., None], t5[_rel_bucket(rel)] - far, 0.0)
    return jnp.transpose(tiles, (0, 3, 1, 2)), far.reshape(-1, 1, 1)


def _run_trunk(x, mods, past, W, LW, t5_table):
    B, T, D = x.shape
    n = B * T
    depth = len(LW)
    n_past = 0 if past is None else past[1].shape[1] * PAGE_SIZE
    rope = _rope_tables(n_past + jnp.arange(T, dtype=jnp.int32))
    if past is None:
        bias_a = _bias_tiles(t5_table[:, :N_HEADS])
        bias_c = _bias_tiles(t5_table[:, N_HEADS:])
    xf = x.reshape(n, D)
    delta = gate = None
    per_layer = []
    for l in range(depth):
        lw = LW[l]
        sh1, sc1, g1, sh2, sc2, g2 = jnp.split(mods[l], 6, axis=-1)
        xf, h = _norm(xf, T, W['norm_mix'][l], delta, gate, sc1, sh1)
        z = _mm(h, lw['w_small'])
        logf, d_lat, d_rope, q_mla, kc = _prep(z, T, lw, rope)
        rows = (z[:, Z_KVA:Z_KVA + 2 * HEAD_DIM].reshape(B, T, 2, HEAD_DIM),
                z[:, Z_KIDX:Z_KIDX + IDX_DIM].reshape(B, T, IDX_DIM),
                z[:, Z_KVB:Z_KVB + 2 * HEAD_DIM].reshape(B, T, 2, HEAD_DIM),
                logf[:, :N_HEADS].reshape(B, T, N_HEADS),
                z[:, Z_KVC:Z_KVC + 2 * HEAD_DIM].reshape(B, T, 2, HEAD_DIM),
                d_lat.reshape(B, T, MLA_KV_RANK),
                d_rope[:, :MLA_ROPE].reshape(B, T, MLA_ROPE))
        if past is None:
            o = _prompt_mixers(z, logf, q_mla, kc, lw, bias_a, bias_c, B, T)
        else:
            assert T == 1
            o = _paged_mixers(z, rows, logf, q_mla, l, lw, t5_table, past[0], past[1], B)
        mixed = _gate_mix(h, lw['w_gates'], o, lw['w_branch'])
        mix = _mm(mixed, lw['w_out'])
        xf, h2 = _norm(xf, T, W['norm_ffn'][l], mix, g1, sc2, sh2)
        delta, gate = _peer(h2, lw), g2
        per_layer.append(rows)
    _, y = _norm(xf, T, W['norm_final'], delta, gate, out_dtype=F32)
    stacked = tuple(jnp.stack([r[i] for r in per_layer], axis=0) for i in range(7))
    return y.reshape(B, T, D), stacked


def kernel(x_prompt, x_sample, cache_a_kv, cache_a_idx, cache_b_kv, cache_b_logf, cache_c_kv,
           cache_d_latent, cache_d_rope, page_table, c_prompt, c_sample, t5_table, w_ada, b_ada,
           norm_mix, norm_ffn, w_in, b_forget, g_cq, g_ckv, w_uq, w_ukv, w_branch, w_out,
           w_peer_q, peer_k1, peer_k2, peer_u, peer_v, norm_final):
    W = {
        'norm_mix': norm_mix, 'norm_ffn': norm_ffn, 'w_in': w_in, 'b_forget': b_forget, 'g_cq': g_cq,
        'g_ckv': g_ckv, 'w_uq': w_uq, 'w_ukv': w_ukv, 'w_branch': w_branch, 'w_out': w_out,
        'w_peer_q': w_peer_q, 'peer_k1': peer_k1, 'peer_k2': peer_k2, 'peer_u': peer_u,
        'peer_v': peer_v, 'norm_final': norm_final,
    }
    caches = {
        'a_kv': cache_a_kv, 'a_idx': cache_a_idx, 'b_kv': cache_b_kv, 'b_logf': cache_b_logf,
        'c_kv': cache_c_kv, 'd_lat': cache_d_latent, 'd_rope': cache_d_rope,
    }
    depth, D = w_in.shape[0], w_in.shape[1]
    n_prompt = c_prompt.shape[0]
    LW = [_layer_weights(l, W, D) for l in range(depth)]
    c_all = jnp.concatenate([c_prompt, c_sample], axis=0)
    mods = [_mm(c_all, w_ada[l], bias=b_ada[l], silu_in=True) for l in range(depth)]
    y_prompt, rows_p = _run_trunk(x_prompt, [m[:n_prompt] for m in mods], None, W, LW, t5_table)
    y_sample, rows_s = _run_trunk(x_sample, [m[n_prompt:] for m in mods], (caches, page_table), W, LW, t5_table)
    return (y_prompt, y_sample) + tuple(rows_p) + tuple(rows_s)
```

```python
import math
from functools import partial

import jax
import jax.numpy as jnp
from jax import lax
from jax.experimental import pallas as pl
from jax.experimental.pallas import tpu as pltpu

PAGE_SIZE = 128
HEAD_DIM = 64
N_HEADS = 8
N_IDX_HEADS = 16
IDX_DIM = 32
DSA_TOPK = 256
MOBA_BLOCK = 256
MOBA_TOPK = 3
MLA_Q_RANK = 384
MLA_KV_RANK = 128
MLA_NOPE = 64
MLA_ROPE = 32
MLA_V = 64
ROPE_THETA = 10000.0
N_BUCKETS = 32
MAX_DISTANCE = 128
N_BRANCH = 4
BRANCH_WIDTH = N_HEADS * HEAD_DIM
PEER_HEADS = 8
PEER_KEY_DIM = 256
PEER_N_KEYS = 128
PEER_TOPK = 16
PEER_SEL = PEER_HEADS * PEER_TOPK
EPS = 1e-6
NEG = -1e30

LANES = 128
VMEM_LIMIT = 56 * 1024 * 1024
TQ = 256
MLA_QW = 256
DSA_FOLD = 8

F32 = jnp.float32
BF16 = jnp.bfloat16

Z_CQ, Z_CKV, Z_KR, Z_FB = 0, 384, 512, 640
Z_PREP_W = 768
Z_QA, Z_KVA, Z_QIDX, Z_KIDX = 768, 1280, 1408, 1920
Z_WIDX = Z_KIDX + IDX_DIM
Z_QB, Z_KVB, Z_QC, Z_KVC = 2048, 2560, 2688, 3200
Z_SMALL = 3328


def _in_offsets(d_model):
    layout = (
        ('q_a', N_HEADS * HEAD_DIM), ('k_a', HEAD_DIM), ('v_a', HEAD_DIM),
        ('q_idx', N_IDX_HEADS * IDX_DIM), ('k_idx', IDX_DIM), ('w_idx', N_IDX_HEADS),
        ('q_b', N_HEADS * HEAD_DIM), ('k_b', HEAD_DIM), ('v_b', HEAD_DIM), ('f_b', N_HEADS),
        ('q_c', N_HEADS * HEAD_DIM), ('k_c', HEAD_DIM), ('v_c', HEAD_DIM),
        ('cq_d', MLA_Q_RANK), ('ckv_d', MLA_KV_RANK), ('kr_d', MLA_ROPE),
        ('gates', N_BRANCH * d_model),
    )
    offs, off = {}, 0
    for name, size in layout:
        offs[name] = (off, size)
        off += size
    return offs


def _cparams(*sem):
    return pltpu.CompilerParams(dimension_semantics=sem, vmem_limit_bytes=VMEM_LIMIT)


def _dot(a, b):
    return jnp.dot(a, b, preferred_element_type=F32)


def _dot_nt(a, b):
    return lax.dot_general(a, b, (((1,), (1,)), ((), ())), preferred_element_type=F32)


def _pick(n, cands):
    for c in cands:
        if n % c == 0:
            return c
    return n


def _mm_kernel(*refs, silu_in, has_bias):
    a_ref, w_ref = refs[0], refs[1]
    o_ref = refs[-1]
    a = a_ref[...]
    if silu_in:
        a = a.astype(F32)
        a = a * jax.nn.sigmoid(a)
    acc = _dot(a.astype(BF16), w_ref[...].astype(BF16))
    if has_bias:
        acc = acc + refs[2][...]
    o_ref[...] = acc.astype(o_ref.dtype)


def _mm(a, w, bias=None, silu_in=False, out_dtype=F32):
    M, K = a.shape
    N = w.shape[1]
    tm = _pick(M, (1024, 512, 256, 128))
    tn = _pick(N, (512, 256, 128))
    in_specs = [pl.BlockSpec((tm, K), lambda i, j: (i, 0)),
                pl.BlockSpec((K, tn), lambda i, j: (0, j))]
    args = [a, w]
    if bias is not None:
        in_specs.append(pl.BlockSpec((1, tn), lambda i, j: (0, j)))
        args.append(bias.reshape(1, N).astype(F32))
    return pl.pallas_call(
        partial(_mm_kernel, silu_in=silu_in, has_bias=bias is not None),
        grid=(M // tm, N // tn),
        in_specs=in_specs,
        out_specs=pl.BlockSpec((tm, tn), lambda i, j: (i, j)),
        out_shape=jax.ShapeDtypeStruct((M, N), out_dtype),
        compiler_params=_cparams("parallel", "parallel"),
    )(*args)


def _norm_kernel(*refs, has_delta, has_mod):
    it = iter(refs)
    x_ref = next(it)
    x = x_ref[...]
    if has_delta:
        d_ref, gt_ref = next(it), next(it)
        x = x + gt_ref[...] * d_ref[...]
    g_ref = next(it)
    if has_mod:
        sc_ref, sh_ref = next(it), next(it)
    if has_delta:
        xo_ref = next(it)
        xo_ref[...] = x
    h_ref = next(it)
    y = x * lax.rsqrt(jnp.mean(x * x, axis=-1, keepdims=True) + EPS) * g_ref[...]
    if has_mod:
        y = y * (1.0 + sc_ref[...]) + sh_ref[...]
    h_ref[...] = y.astype(h_ref.dtype)


def _norm(x, T, g, delta=None, gate=None, sc=None, sh=None, out_dtype=BF16):
    n, D = x.shape
    tr = _pick(n, (256, 128)) if T > 1 else n
    if T > 1:
        assert T % tr == 0

    def row_spec():
        return pl.BlockSpec((tr, D), lambda i: (i, 0))

    def mod_arg(a):
        if T == 1:
            return a, row_spec()
        return a[:, None, :], pl.BlockSpec((None, 1, D), lambda i: (i * tr // T, 0, 0))

    has_delta, has_mod = delta is not None, sc is not None
    args, specs = [x], [row_spec()]
    if has_delta:
        ga, gs = mod_arg(gate)
        args += [delta, ga]
        specs += [row_spec(), gs]
    args.append(g.reshape(1, D))
    specs.append(pl.BlockSpec((1, D), lambda i: (0, 0)))
    if has_mod:
        for a in (sc, sh):
            aa, ss = mod_arg(a)
            args.append(aa)
            specs.append(ss)
    out_shape, out_specs = [], []
    if has_delta:
        out_shape.append(jax.ShapeDtypeStruct((n, D), F32))
        out_specs.append(row_spec())
    out_shape.append(jax.ShapeDtypeStruct((n, D), out_dtype))
    out_specs.append(row_spec())
    outs = pl.pallas_call(
        partial(_norm_kernel, has_delta=has_delta, has_mod=has_mod),
        grid=(n // tr,), in_specs=specs, out_specs=out_specs, out_shape=out_shape,
        compiler_params=_cparams("parallel"),
    )(*args)
    if has_delta:
        return outs[0], outs[1]
    return x, outs[0]


def _prep_kernel(z_ref, gcq_ref, gckv_ref, bf_ref, wq_ref, wuk_ref, c128_ref, s128_ref, cq_ref, sq_ref,
                 logf_ref, dlat_ref, drope_ref, qm_ref, kc_ref):
    nope_w = N_HEADS * MLA_NOPE
    qw = N_HEADS * MLA_QW
    cq = z_ref[:, Z_CQ:Z_CQ + MLA_Q_RANK]
    ckv = z_ref[:, Z_CKV:Z_CKV + MLA_KV_RANK]
    kr = z_ref[:, Z_KR:Z_KR + LANES]
    fb = z_ref[:, Z_FB:Z_FB + LANES]

    def rms(x, g):
        return x * lax.rsqrt(jnp.mean(x * x, axis=-1, keepdims=True) + EPS) * g

    cqn = rms(cq, gcq_ref[...]).astype(BF16)
    qall = _dot(cqn, wq_ref[...])
    q_nope = qall[:, :nope_w].astype(BF16)
    q_r = qall[:, nope_w:nope_w + qw]
    q_rs = qall[:, nope_w + qw:]
    q_lat = _dot(q_nope, wuk_ref[...])
    cos_q = jnp.concatenate([cq_ref[...]] * N_HEADS, axis=1)
    sin_q = jnp.concatenate([sq_ref[...]] * N_HEADS, axis=1)
    scale = (MLA_NOPE + MLA_ROPE) ** -0.5
    qm_ref[...] = ((q_lat + q_r * cos_q + q_rs * sin_q) * scale).astype(BF16)

    dlat = rms(ckv, gckv_ref[...])
    dlat_ref[...] = dlat
    kr_sw = pltpu.roll(kr, LANES - MLA_ROPE, axis=1)
    drope = kr * c128_ref[...] + kr_sw * s128_ref[...]
    drope_ref[...] = drope
    kc_ref[...] = jnp.concatenate([dlat, drope], axis=1).astype(BF16)

    xf = fb + bf_ref[...]
    logf_ref[...] = jnp.minimum(xf, 0.0) - jnp.log1p(jnp.exp(-jnp.abs(xf)))


def _prep(z, T, lw, rope):
    n = z.shape[0]
    tm = _pick(n, (512, 256, 128))
    c128, s128, cq, sq = rope
    if T == 1:
        def tab_spec(w):
            return pl.BlockSpec((1, w), lambda i: (0, 0))
    else:
        assert T % tm == 0
        nt = T // tm

        def tab_spec(w):
            return pl.BlockSpec((tm, w), lambda i: (i % nt, 0))

    def full(a):
        return pl.BlockSpec(a.shape, lambda i: (0,) * a.ndim)

    def rows(w):
        return pl.BlockSpec((tm, w), lambda i: (i, 0))

    return pl.pallas_call(
        _prep_kernel,
        grid=(n // tm,),
        in_specs=[pl.BlockSpec((tm, Z_PREP_W), lambda i: (i, 0)),
                  full(lw['g_cq']), full(lw['g_ckv']), full(lw['b_forget']), full(lw['wq_all']),
                  full(lw['wuk_bd']), tab_spec(LANES), tab_spec(LANES), tab_spec(MLA_QW), tab_spec(MLA_QW)],
        out_specs=[rows(LANES), rows(LANES), rows(LANES), rows(N_HEADS * MLA_QW), rows(MLA_QW)],
        out_shape=[jax.ShapeDtypeStruct((n, LANES), F32), jax.ShapeDtypeStruct((n, LANES), F32),
                   jax.ShapeDtypeStruct((n, LANES), F32),
                   jax.ShapeDtypeStruct((n, N_HEADS * MLA_QW), BF16),
                   jax.ShapeDtypeStruct((n, MLA_QW), BF16)],
        compiler_params=_cparams("parallel"),
    )(z, lw['g_cq'], lw['g_ckv'], lw['b_forget'], lw['wq_all'], lw['wuk_bd'], c128, s128, cq, sq)


CUM_CHUNK = 256


def _cumsum_kernel(x_ref, o_ref):
    T = x_ref.shape[0]
    r = lax.broadcasted_iota(jnp.int32, (CUM_CHUNK, CUM_CHUNK), 0)
    c = lax.broadcasted_iota(jnp.int32, (CUM_CHUNK, CUM_CHUNK), 1)
    tri = jnp.where(r >= c, 1.0, 0.0).astype(BF16)

    def body(i, carry):
        off = pl.multiple_of(i * CUM_CHUNK, CUM_CHUNK)
        x = x_ref[pl.ds(off, CUM_CHUNK), :]
        hi = x.astype(BF16)
        r1 = x - hi.astype(F32)
        mid = r1.astype(BF16)
        lo = (r1 - mid.astype(F32)).astype(BF16)
        f = _dot(tri, hi) + _dot(tri, mid) + _dot(tri, lo) + carry
        o_ref[pl.ds(off, CUM_CHUNK), :] = f
        return f[CUM_CHUNK - 1:CUM_CHUNK, :]

    lax.fori_loop(0, T // CUM_CHUNK, body, jnp.zeros((1, LANES), F32))


def _cumsum(x):
    B, T, _ = x.shape
    assert T % CUM_CHUNK == 0
    spec = pl.BlockSpec((None, T, LANES), lambda b: (b, 0, 0))
    return pl.pallas_call(
        _cumsum_kernel, grid=(B,), in_specs=[spec], out_specs=spec,
        out_shape=jax.ShapeDtypeStruct(x.shape, F32), compiler_params=_cparams("parallel"),
    )(x)


def _softmax_pv(s, v):
    m = jnp.max(s, axis=-1, keepdims=True)
    p = jnp.exp(s - m)
    l = jnp.sum(p, axis=-1, keepdims=True)
    return _dot(p.astype(BF16), v) / l


def _causal(qi, T):
    row = lax.broadcasted_iota(jnp.int32, (TQ, T), 0) + qi * TQ
    col = lax.broadcasted_iota(jnp.int32, (TQ, T), 1)
    return row, col


def _fox_kernel(q_ref, k_ref, v_ref, fq_ref, fk_ref, o_ref):
    qi = pl.program_id(1)
    k, v = k_ref[...], v_ref[...]
    row, col = _causal(qi, k.shape[0])
    causal = col <= row

    def body(h, _):
        s = _dot_nt(q_ref[h], k) + fq_ref[h] - fk_ref[h]
        o_ref[h] = _softmax_pv(jnp.where(causal, s, NEG), v)
        return 0

    lax.fori_loop(0, N_HEADS, body, 0)


def _mla_kernel(q_ref, kc_ref, wuv_ref, o_ref):
    qi = pl.program_id(1)
    kc = kc_ref[...]
    v = kc[:, :MLA_KV_RANK]
    row, col = _causal(qi, kc.shape[0])
    causal = col <= row

    def body(h, _):
        s = _dot_nt(q_ref[h], kc)
        o_lat = _softmax_pv(jnp.where(causal, s, NEG), v)
        o_ref[h] = _dot(o_lat.astype(BF16), wuv_ref[h])
        return 0

    lax.fori_loop(0, N_HEADS, body, 0)


def _biased_logits(q, k, h, qi, dt_ref, c_ref, s_ref):
    off = pl.multiple_of(qi * TQ, TQ)
    s_ref[...] = _dot_nt(q, k) + c_ref[h]
    s_ref[:, pl.ds(off, TQ)] += dt_ref[0, h]

    @pl.when(qi > 0)
    def _():
        s_ref[:, pl.ds(off - TQ, TQ)] += dt_ref[1, h]

    return s_ref[...]


def _moba_kernel(q_ref, k_ref, v_ref, kf_ref, dt_ref, c_ref, o_ref, s_ref, *, k_sel):
    qi = pl.program_id(1)
    k, v = k_ref[...], v_ref[...]
    T = k.shape[0]
    nb = T // MOBA_BLOCK
    km = jnp.mean(kf_ref[...].reshape(nb, MOBA_BLOCK, HEAD_DIM), axis=1)
    km = jnp.concatenate([km, jnp.zeros((LANES - nb, HEAD_DIM), F32)], axis=0).astype(BF16)
    row, col = _causal(qi, T)
    own = (col >= qi * TQ) & (col <= row)
    lane = lax.broadcasted_iota(jnp.int32, (TQ, LANES), 1)
    eb = lax.broadcasted_iota(jnp.int32, (LANES, T), 0)
    es = lax.broadcasted_iota(jnp.int32, (LANES, T), 1)
    expand = jnp.where(es // MOBA_BLOCK == eb, 1.0, 0.0).astype(BF16)

    def body(h, _):
        q = q_ref[h]
        gate = _dot_nt(q, km)
        rank = jnp.zeros((TQ, LANES), F32)
        for m in range(nb):
            gm = gate[:, m:m + 1]
            beats = (gm > gate) | ((gm == gate) & (lane > m))
            rank = rank + jnp.where(beats, jnp.where(m < qi, 1.0, 0.0), 0.0)
        selb = jnp.where((rank < k_sel) & (lane < qi), 1.0, 0.0).astype(BF16)
        picked = _dot(selb, expand) > 0.5
        s = _biased_logits(q, k, h, qi, dt_ref, c_ref, s_ref)
        o_ref[h] = _softmax_pv(jnp.where(picked | own, s, NEG), v)
        return 0

    lax.fori_loop(0, N_HEADS, body, 0)


def _sortable(x):
    b = pltpu.bitcast(x, jnp.int32)
    return b ^ ((b >> 31) & jnp.int32(0x7FFFFFFF))


def _topk_mask(score, col, k_sel, n_idx_bits, tail=None, flat=False):
    key = _sortable(score)
    tkey = None if tail is None else _sortable(tail)
    kf = float(k_sel)

    def _count(mask):
        n = jnp.sum(jnp.where(mask, 1.0, 0.0), axis=-1, keepdims=True)
        return jnp.sum(n, axis=0, keepdims=True) if flat else n

    def count_ge(c):
        n = _count(key >= c)
        return n if tkey is None else n + jnp.where(tkey >= c, 1.0, 0.0)

    prefix = jnp.where(count_ge(jnp.int32(0)) >= kf, jnp.int32(0), jnp.int32(-2 ** 31))

    def vbody(i, prefix):
        cand = prefix | (jnp.int32(1) << (30 - i))
        return jnp.where(count_ge(cand) >= kf, cand, prefix)

    thr = lax.fori_loop(0, 31, vbody, prefix)
    above = key > thr
    tie = key == thr
    n_above = _count(above)
    if tkey is not None:
        n_above = n_above + jnp.where(tkey > thr, 1.0, 0.0)
    need = kf - n_above

    def ibody(i, p):
        cand = p | (jnp.int32(1) << (n_idx_bits - 1 - i))
        return jnp.where(_count(tie & (col < cand)) < need, cand, p)

    last = lax.fori_loop(0, n_idx_bits, ibody, jnp.zeros_like(thr))
    mask = above | (tie & (col <= last))
    if tkey is None:
        return mask
    return mask, (tkey > thr) | ((tkey == thr) & (_count(tie) < need))


def _dsa_kernel(q_ref, qi_ref, w_ref, kidx_ref, k_ref, v_ref, dt_ref, c_ref, o_ref, s_ref, sc_ref, *, k_sel):
    qi = pl.program_id(1)
    k, v, kidx = k_ref[...], v_ref[...], kidx_ref[...]
    T = k.shape[0]
    row, col = _causal(qi, T)
    causal = col <= row

    sc_ref[...] = jnp.zeros_like(sc_ref)

    def ibody(h, _):
        sc_ref[...] += w_ref[h] * jnp.maximum(_dot_nt(qi_ref[h], kidx), 0.0)
        return 0

    lax.fori_loop(0, N_IDX_HEADS, ibody, 0)
    score = jnp.where(causal, sc_ref[...], NEG)
    mask = _topk_mask(score, col, k_sel, max(1, (T - 1).bit_length())) & causal

    def body(h, _):
        s = _biased_logits(q_ref[h], k, h, qi, dt_ref, c_ref, s_ref)
        o_ref[h] = _softmax_pv(jnp.where(mask, s, NEG), v)
        return 0

    lax.fori_loop(0, N_HEADS, body, 0)


def _heads_spec(nh, w):
    return pl.BlockSpec((None, nh, TQ, w), lambda b, i: (b, 0, i, 0))


def _ctx_spec(T, w):
    return pl.BlockSpec((None, T, w), lambda b, i: (b, 0, 0))


def _full_spec(a):
    return pl.BlockSpec(a.shape, lambda b, i: (0,) * a.ndim)


def _attn_call(kern, B, T, in_specs, args, scratch=()):
    return pl.pallas_call(
        kern, grid=(B, T // TQ), in_specs=in_specs,
        out_specs=_heads_spec(N_HEADS, HEAD_DIM),
        out_shape=jax.ShapeDtypeStruct((B, N_HEADS, T, HEAD_DIM), F32),
        scratch_shapes=list(scratch),
        compiler_params=_cparams("parallel", "arbitrary"),
    )(*args)


def _to_heads(x, B, T, nh, d, scale=1.0):
    return (x.reshape(B, T, nh, d) * scale).astype(BF16).transpose(0, 2, 1, 3)


def _from_heads(o):
    B, nh, T, d = o.shape
    return o.transpose(0, 2, 1, 3).reshape(B * T, nh * d).astype(BF16)


def _prompt_mixers(z, logf, q_mla, kc, lw, bias_a, bias_c, B, T):
    assert T % TQ == 0 and TQ == MOBA_BLOCK and MAX_DISTANCE <= TQ
    scale = HEAD_DIM ** -0.5

    def ctx(lo, w):
        return z[:, lo:lo + w].reshape(B, T, w)

    sblock = pltpu.VMEM((TQ, T), F32)
    q_a = _to_heads(z[:, Z_QA:Z_QA + BRANCH_WIDTH], B, T, N_HEADS, HEAD_DIM, scale)
    q_idx = _to_heads(z[:, Z_QIDX:Z_QIDX + N_IDX_HEADS * IDX_DIM], B, T, N_IDX_HEADS, IDX_DIM)
    w_idx = (z[:, Z_WIDX:Z_WIDX + N_IDX_HEADS] * N_IDX_HEADS ** -0.5).reshape(B, T, N_IDX_HEADS)
    w_idx = w_idx.transpose(0, 2, 1)[..., None]
    k_idx = ctx(Z_KIDX, IDX_DIM).astype(BF16)
    ka = ctx(Z_KVA, HEAD_DIM).astype(BF16)
    va = ctx(Z_KVA + HEAD_DIM, HEAD_DIM).astype(BF16)
    dt_a, c_a = bias_a
    o_a = _attn_call(
        partial(_dsa_kernel, k_sel=min(DSA_TOPK, T // 4)), B, T,
        [_heads_spec(N_HEADS, HEAD_DIM), _heads_spec(N_IDX_HEADS, IDX_DIM), _heads_spec(N_IDX_HEADS, 1),
         _ctx_spec(T, IDX_DIM), _ctx_spec(T, HEAD_DIM), _ctx_spec(T, HEAD_DIM), _full_spec(dt_a), _full_spec(c_a)],
        (q_a, q_idx, w_idx, k_idx, ka, va, dt_a, c_a), scratch=(sblock, sblock))
    q_b = _to_heads(z[:, Z_QB:Z_QB + BRANCH_WIDTH], B, T, N_HEADS, HEAD_DIM, scale)
    kb = ctx(Z_KVB, HEAD_DIM).astype(BF16)
    vb = ctx(Z_KVB + HEAD_DIM, HEAD_DIM).astype(BF16)
    f_cum = _cumsum(logf.reshape(B, T, LANES))[:, :, :N_HEADS].transpose(0, 2, 1)
    o_b = _attn_call(
        _fox_kernel, B, T,
        [_heads_spec(N_HEADS, HEAD_DIM), _ctx_spec(T, HEAD_DIM), _ctx_spec(T, HEAD_DIM), _heads_spec(N_HEADS, 1),
         pl.BlockSpec((None, N_HEADS, 1, T), lambda b, i: (b, 0, 0, 0))],
        (q_b, kb, vb, f_cum[..., None], f_cum[:, :, None, :]))
    q_c = _to_heads(z[:, Z_QC:Z_QC + BRANCH_WIDTH], B, T, N_HEADS, HEAD_DIM, scale)
    kc_f = ctx(Z_KVC, HEAD_DIM)
    vc = ctx(Z_KVC + HEAD_DIM, HEAD_DIM).astype(BF16)
    dt_c, c_c = bias_c
    o_c = _attn_call(
        partial(_moba_kernel, k_sel=min(MOBA_TOPK, T // MOBA_BLOCK)), B, T,
        [_heads_spec(N_HEADS, HEAD_DIM), _ctx_spec(T, HEAD_DIM), _ctx_spec(T, HEAD_DIM), _ctx_spec(T, HEAD_DIM),
         _full_spec(dt_c), _full_spec(c_c)],
        (q_c, kc_f.astype(BF16), vc, kc_f, dt_c, c_c), scratch=(sblock,))
    q_d = q_mla.reshape(B, T, N_HEADS, MLA_QW).transpose(0, 2, 1, 3)
    o_d = _attn_call(
        _mla_kernel, B, T,
        [_heads_spec(N_HEADS, MLA_QW), _ctx_spec(T, MLA_QW), _full_spec(lw['w_uv'])],
        (q_d, kc.reshape(B, T, MLA_QW), lw['w_uv']))
    return jnp.stack([_from_heads(o) for o in (o_a, o_b, o_c, o_d)], axis=0)


def _gate_mix_kernel(h_ref, g0_ref, g1_ref, g2_ref, g3_ref, o_ref, wb_ref, out_ref):
    h = h_ref[...]
    acc = None
    for n, g_ref in enumerate((g0_ref, g1_ref, g2_ref, g3_ref)):
        gate = jax.nn.sigmoid(_dot(h, g_ref[...]))
        term = gate * _dot(o_ref[n], wb_ref[n])
        acc = term if acc is None else acc + term
    out_ref[...] = acc.astype(out_ref.dtype)


def _gate_mix(h, w_gates, o, w_branch):
    n, D = h.shape
    tm = _pick(n, (1024, 512, 256, 128))
    td = _pick(D, (256, 128))
    nd = D // td
    gate_specs = [pl.BlockSpec((D, td), partial(lambda i, j, b: (0, b * nd + j), b=b)) for b in range(N_BRANCH)]
    return pl.pallas_call(
        _gate_mix_kernel,
        grid=(n // tm, nd),
        in_specs=[pl.BlockSpec((tm, D), lambda i, j: (i, 0))] + gate_specs + [
            pl.BlockSpec((N_BRANCH, tm, BRANCH_WIDTH), lambda i, j: (0, i, 0)),
            pl.BlockSpec((N_BRANCH, BRANCH_WIDTH, td), lambda i, j: (0, 0, j))],
        out_specs=pl.BlockSpec((tm, td), lambda i, j: (i, j)),
        out_shape=jax.ShapeDtypeStruct((n, D), BF16),
        compiler_params=_cparams("parallel", "parallel"),
    )(h, w_gates, w_gates, w_gates, w_gates, o, w_branch)


def _top16(s):
    R, tn = s.shape
    rid = lax.broadcasted_iota(jnp.int32, (R, tn), 0)
    kid = lax.broadcasted_iota(jnp.int32, (PEER_TOPK, tn), 0)

    def body(k, carry):
        s, vals, idxs = carry
        m = jnp.max(s, axis=0, keepdims=True)
        i = jnp.min(jnp.where(s == m, rid, R), axis=0, keepdims=True)
        vals = jnp.where(kid == k, m, vals)
        idxs = jnp.where(kid == k, i, idxs)
        return jnp.where(rid == i, -jnp.inf, s), vals, idxs

    _, vals, idxs = lax.fori_loop(
        0, PEER_TOPK, body, (s, jnp.zeros((PEER_TOPK, tn), F32), jnp.zeros((PEER_TOPK, tn), jnp.int32)))
    return vals, idxs


def _peer_select_kernel(q_ref, k1_ref, k2_ref, e1_ref, e2_ref, g_ref):
    half = PEER_KEY_DIM // 2
    k1, k2 = k1_ref[...], k2_ref[...]

    def pick(idx, table):
        out = jnp.zeros_like(idx)
        for j in range(PEER_TOPK):
            out = out + jnp.where(idx == j, table[j:j + 1, :], 0)
        return out

    def body(h, _):
        off = pl.multiple_of(h * PEER_KEY_DIM, PEER_KEY_DIM)
        q1 = q_ref[:, pl.ds(off, half)].astype(BF16)
        q2 = q_ref[:, pl.ds(off + half, half)].astype(BF16)
        v1, i1 = _top16(_dot_nt(k1, q1))
        v2, i2 = _top16(_dot_nt(k2, q2))
        cand = jnp.concatenate([v1[a:a + 1, :] + v2 for a in range(PEER_TOPK)], axis=0)
        top, ci = _top16(cand)
        e = jnp.exp(top - jnp.max(top, axis=0, keepdims=True))
        rows = pl.ds(pl.multiple_of(h * PEER_TOPK, PEER_TOPK), PEER_TOPK)
        e1_ref[rows, :] = pick(ci // PEER_TOPK, i1)
        e2_ref[rows, :] = pick(ci % PEER_TOPK, i2)
        g_ref[rows, :] = e / jnp.sum(e, axis=0, keepdims=True)
        return 0

    lax.fori_loop(0, PEER_HEADS, body, 0)


def _peer_select(q, k1, k2):
    n = q.shape[0]
    tn = _pick(n, (256, 128))
    out_spec = pl.BlockSpec((PEER_SEL, tn), lambda i: (0, i))
    kspec = pl.BlockSpec(k1.shape, lambda i: (0, 0))
    return pl.pallas_call(
        _peer_select_kernel, grid=(n // tn,),
        in_specs=[pl.BlockSpec((tn, PEER_HEADS * PEER_KEY_DIM), lambda i: (i, 0)), kspec, kspec],
        out_specs=[out_spec] * 3,
        out_shape=[jax.ShapeDtypeStruct((PEER_SEL, n), jnp.int32)] * 2 + [jax.ShapeDtypeStruct((PEER_SEL, n), F32)],
        compiler_params=_cparams("parallel"),
    )(q, k1, k2)


PEER_G_TOKENS = 32


def _peer_gates_kernel(e1_ref, e2_ref, g_ref, o_ref):
    shape = (PEER_G_TOKENS, PEER_N_KEYS, PEER_SEL)
    rid = lax.broadcasted_iota(jnp.int32, shape, 1)
    p1 = jnp.where(e1_ref[...][:, None, :] == rid, 1.0, 0.0).astype(BF16)
    gp2 = jnp.where(e2_ref[...][:, None, :] == rid, g_ref[...][:, None, :], 0.0).astype(BF16)
    o_ref[...] = jnp.einsum('nrk,nck->nrc', p1, gp2, preferred_element_type=F32).astype(o_ref.dtype)


def _peer_gates(e1, e2, g):
    n = e1.shape[0]
    spec = pl.BlockSpec((PEER_G_TOKENS, PEER_SEL), lambda i: (i, 0))
    return pl.pallas_call(
        _peer_gates_kernel, grid=(n // PEER_G_TOKENS,),
        in_specs=[spec, spec, spec],
        out_specs=pl.BlockSpec((PEER_G_TOKENS, PEER_N_KEYS, PEER_N_KEYS), lambda i: (i, 0, 0)),
        out_shape=jax.ShapeDtypeStruct((n, PEER_N_KEYS, PEER_N_KEYS), BF16),
        compiler_params=_cparams("parallel"),
    )(e1, e2, g)


def _peer_dense_kernel(h_ref, u_ref, g_ref, v_ref, o_ref):
    @pl.when(pl.program_id(1) == 0)
    def _():
        o_ref[...] = jnp.zeros_like(o_ref)

    a = _dot_nt(h_ref[...], u_ref[...])
    act = 0.5 * a * (1.0 + lax.erf(a * (0.5 ** 0.5)))
    w = (g_ref[...].astype(F32) * act).astype(BF16)
    o_ref[...] += _dot(w, v_ref[...])


def _peer_dense(h, u, g, v):
    n, D = h.shape
    E = u.shape[0]
    tn = _pick(n, (512, 256, 128))
    te = _pick(E, (512, 256, 128))
    return pl.pallas_call(
        _peer_dense_kernel, grid=(n // tn, E // te),
        in_specs=[pl.BlockSpec((tn, D), lambda i, j: (i, 0)), pl.BlockSpec((te, D), lambda i, j: (j, 0)),
                  pl.BlockSpec((tn, te), lambda i, j: (i, j)), pl.BlockSpec((te, D), lambda i, j: (j, 0))],
        out_specs=pl.BlockSpec((tn, D), lambda i, j: (i, 0)),
        out_shape=jax.ShapeDtypeStruct((n, D), F32),
        compiler_params=_cparams("parallel", "arbitrary"),
    )(h, u, g, v)


def _peer(h, lw):
    q = _mm(h, lw['w_peer_q'])
    e1, e2, g = _peer_select(q, lw['peer_k1'], lw['peer_k2'])
    grid = _peer_gates(e1.T, e2.T, g.T)
    return _peer_dense(h, lw['peer_u'], grid.reshape(h.shape[0], PEER_N_KEYS * PEER_N_KEYS), lw['peer_v'])


def _rel_bucket(rel):
    n_exact = N_BUCKETS // 2
    relf = jnp.maximum(rel, 1).astype(F32)
    large = n_exact + (jnp.log(relf / n_exact) / math.log(MAX_DISTANCE / n_exact)
                       * (N_BUCKETS - n_exact)).astype(jnp.int32)
    large = jnp.minimum(large, N_BUCKETS - 1)
    return jnp.where(rel < n_exact, jnp.maximum(rel, 0), large)


def _dec_softmax(s, s_new, vmat, v_new, transposed=True):
    m = jnp.maximum(jnp.max(s, axis=-1, keepdims=True), s_new)
    p = jnp.exp(s - m)
    pn = jnp.exp(s_new - m)
    l = jnp.sum(p, axis=-1, keepdims=True) + pn
    pv = _dot_nt(p.astype(BF16), vmat) if transposed else _dot(p.astype(BF16), vmat)
    return (pv + pn * v_new) / l


def _new_logit(q, new_row):
    return jnp.sum(q.astype(F32) * new_row.astype(BF16).astype(F32), axis=-1, keepdims=True)


def _pages_t(ref):
    return jnp.concatenate([ref[p] for p in range(ref.shape[0])], axis=1)


def _fox_dec_kernel(q_ref, kv_ref, new_ref, fk_ref, ln_ref, o_ref):
    q, new = q_ref[...].astype(BF16), new_ref[...]
    kvt = _pages_t(kv_ref).astype(BF16)
    fk = fk_ref[...]
    f_t = fk[:, fk.shape[1] - 1:] + ln_ref[...]
    s = _dot(q, kvt) + (f_t - fk)
    o = _dec_softmax(s, _new_logit(q, new), kvt, new)
    o_ref[...] = o[:, HEAD_DIM:]


def _dsa_dec_kernel(q_ref, qi_ref, w_ref, kidx_ref, kinew_ref, kv_ref, new_ref, bias_ref, bnew_ref, o_ref, *, k_sel):
    q, new, w = q_ref[...].astype(BF16), new_ref[...], w_ref[...]
    qidx = qi_ref[...].astype(BF16)
    dots = jnp.maximum(_dot(qidx, _pages_t(kidx_ref).astype(BF16)), 0.0)
    score = jnp.sum(w * dots, axis=0, keepdims=True)
    score_new = jnp.sum(w * jnp.maximum(_new_logit(qidx, kinew_ref[...]), 0.0), axis=0, keepdims=True)
    Lp = score.shape[1]
    chunk = Lp // DSA_FOLD
    folded = jnp.concatenate([score[:, r * chunk:(r + 1) * chunk] for r in range(DSA_FOLD)], axis=0)
    col = (lax.broadcasted_iota(jnp.int32, folded.shape, 0) * chunk
           + lax.broadcasted_iota(jnp.int32, folded.shape, 1))
    picked, mask_new = _topk_mask(folded, col, k_sel, max(1, (Lp - 1).bit_length()), tail=score_new, flat=True)
    picked = jnp.where(picked, 1.0, 0.0)
    mask = jnp.concatenate([picked[r:r + 1, :] for r in range(DSA_FOLD)], axis=1) > 0.5
    kvt = _pages_t(kv_ref).astype(BF16)
    s = jnp.where(mask, _dot(q, kvt) + bias_ref[...], NEG)
    s_new = jnp.where(mask_new, _new_logit(q, new) + bnew_ref[...], NEG)
    o_ref[...] = _dec_softmax(s, s_new, kvt, new)[:, HEAD_DIM:]


def _moba_dec_kernel(q_ref, kv_ref, new_ref, exp_ref, bias_ref, bnew_ref, o_ref, *, k_sel):
    q, new = q_ref[...].astype(BF16), new_ref[...]
    kv = _pages_t(kv_ref)
    nb = kv.shape[1] // MOBA_BLOCK
    expand = exp_ref[...]
    kvt = kv.astype(BF16)
    rest = (kv - kvt.astype(F32)).astype(BF16)
    km = (_dot_nt(kvt, expand) + _dot_nt(rest, expand)) * (1.0 / MOBA_BLOCK)
    gate = _dot(q, km.astype(BF16))
    lane = lax.broadcasted_iota(jnp.int32, gate.shape, 1)
    rank = jnp.zeros(gate.shape, F32)
    for m in range(nb):
        gm = gate[:, m:m + 1]
        rank = rank + jnp.where((gm > gate) | ((gm == gate) & (lane > m)), 1.0, 0.0)
    selb = jnp.where((rank < k_sel) & (lane < nb), 1.0, 0.0).astype(BF16)
    picked = _dot(selb, expand) > 0.5
    s = jnp.where(picked, _dot(q, kvt) + bias_ref[...], NEG)
    o_ref[...] = _dec_softmax(s, _new_logit(q, new) + bnew_ref[...], kvt, new)[:, HEAD_DIM:]


def _mla_dec_kernel(ql_ref, qr_ref, lat_ref, rope_ref, lnew_ref, rnew_ref, o_ref):
    ql, qr = ql_ref[...].astype(BF16), qr_ref[...].astype(BF16)
    latb = lat_ref[...].astype(BF16)
    s = _dot_nt(ql, latb) + _dot(qr, _pages_t(rope_ref).astype(BF16))
    s_new = _new_logit(ql, lnew_ref[...]) + _new_logit(qr, rnew_ref[...])
    o_ref[...] = _dec_softmax(s, s_new, latb, lnew_ref[...], transposed=False)


def _head_proj_kernel(x_ref, w_ref, o_ref):
    o_ref[...] = _dot(x_ref[...].astype(BF16), w_ref[...])


def _head_proj(x, w):
    H, B, c = x.shape
    n = w.shape[2]
    return pl.pallas_call(
        _head_proj_kernel, grid=(H,),
        in_specs=[pl.BlockSpec((None, B, c), lambda h: (h, 0, 0)), pl.BlockSpec((None, c, n), lambda h: (h, 0, 0))],
        out_specs=pl.BlockSpec((None, B, n), lambda h: (h, 0, 0)),
        out_shape=jax.ShapeDtypeStruct((H, B, n), F32),
        compiler_params=_cparams("parallel"),
    )(x, w)


def _dec_call(kern, B, in_specs, args, width=HEAD_DIM):
    return pl.pallas_call(
        kern, grid=(B,), in_specs=in_specs,
        out_specs=pl.BlockSpec((None, N_HEADS, width), lambda b: (b, 0, 0)),
        out_shape=jax.ShapeDtypeStruct((B, N_HEADS, width), F32),
        compiler_params=_cparams("parallel"),
    )(*args)


def _per_seq(a):
    return pl.BlockSpec((None,) + a.shape[1:], lambda b: (b,) + (0,) * (a.ndim - 1))


def _shared(a):
    return pl.BlockSpec(a.shape, lambda b: (0,) * a.ndim)


def _paged_mixers(z, rows, logf, q_mla, l, lw, t5_table, caches, page_table, B):
    n_pages = page_table.shape[1]
    Lp = n_pages * PAGE_SIZE
    assert Lp % MOBA_BLOCK == 0 and Lp // MOBA_BLOCK <= LANES and MAX_DISTANCE <= Lp
    scale = HEAD_DIM ** -0.5

    def past_t(name, w):
        c = caches[name]
        pool = jnp.moveaxis(c, 2, -1).reshape(c.shape[0] * c.shape[1], w, PAGE_SIZE)
        return pool[page_table + l * c.shape[1]]

    def past_rows(name, w):
        c = caches[name]
        return c.reshape(c.shape[0] * c.shape[1], PAGE_SIZE, w)[page_table + l * c.shape[1]].reshape(B, Lp, w)

    def q_pad(lo):
        q = z[:, lo:lo + BRANCH_WIDTH].reshape(B, N_HEADS, HEAD_DIM) * scale
        return jnp.pad(q, ((0, 0), (0, 0), (0, HEAD_DIM)))

    def new_row(lo, w):
        return z[:, lo:lo + w].reshape(B, 1, w)

    def bias_rows(t5):
        rel = Lp - jnp.arange(Lp, dtype=jnp.int32)
        return t5[_rel_bucket(rel)].T, t5[0].reshape(-1, 1)

    kv_w = 2 * HEAD_DIM
    q_a = q_pad(Z_QA)
    q_idx = z[:, Z_QIDX:Z_QIDX + N_IDX_HEADS * IDX_DIM].reshape(B, N_IDX_HEADS, IDX_DIM)
    w_idx = (z[:, Z_WIDX:Z_WIDX + N_IDX_HEADS] * N_IDX_HEADS ** -0.5).reshape(B, N_IDX_HEADS, 1)
    args = (q_a, q_idx, w_idx, past_t('a_idx', IDX_DIM), new_row(Z_KIDX, IDX_DIM), past_t('a_kv', kv_w),
            new_row(Z_KVA, kv_w)) + bias_rows(t5_table[:, :N_HEADS])
    o_a = _dec_call(partial(_dsa_dec_kernel, k_sel=min(DSA_TOPK, (Lp + 1) // 4)), B,
                    [_per_seq(a) for a in args[:7]] + [_shared(a) for a in args[7:]], args)
    lf = past_t('b_logf', N_HEADS).transpose(1, 3, 0, 2).reshape(Lp, B * N_HEADS // LANES, LANES).transpose(1, 0, 2)
    f_past = _cumsum(lf).transpose(1, 0, 2).reshape(Lp, B, N_HEADS).transpose(1, 2, 0)
    args = (q_pad(Z_QB), past_t('b_kv', kv_w), new_row(Z_KVB, kv_w), f_past, logf[:, :N_HEADS].reshape(B, N_HEADS, 1))
    o_b = _dec_call(_fox_dec_kernel, B, [_per_seq(a) for a in args], args)
    blk = jnp.arange(LANES, dtype=jnp.int32)[:, None]
    expand = (jnp.arange(Lp, dtype=jnp.int32)[None, :] // MOBA_BLOCK == blk).astype(BF16)
    args = (q_pad(Z_QC), past_t('c_kv', kv_w), new_row(Z_KVC, kv_w), expand) + bias_rows(t5_table[:, N_HEADS:])
    o_c = _dec_call(partial(_moba_dec_kernel, k_sel=min(MOBA_TOPK, Lp // MOBA_BLOCK + 1)), B,
                    [_per_seq(a) for a in args[:3]] + [_shared(a) for a in args[3:]], args)
    qm = q_mla.reshape(B, N_HEADS, MLA_QW).astype(F32)
    d_lat, d_rope = rows[5], rows[6]
    args = (qm[..., :MLA_KV_RANK], qm[..., MLA_KV_RANK:MLA_KV_RANK + MLA_ROPE], past_rows('d_lat', MLA_KV_RANK),
            past_t('d_rope', MLA_ROPE), d_lat.reshape(B, 1, MLA_KV_RANK), d_rope.reshape(B, 1, MLA_ROPE))
    o_lat = _dec_call(_mla_dec_kernel, B, [_per_seq(a) for a in args], args, width=MLA_KV_RANK)
    o_d = _head_proj(o_lat.transpose(1, 0, 2), lw['w_uv']).transpose(1, 0, 2)
    return jnp.stack([o.reshape(B, BRANCH_WIDTH).astype(BF16) for o in (o_a, o_b, o_c, o_d)], axis=0)


def _layer_weights(l, W, D):
    offs = _in_offsets(D)
    w_in = W['w_in'][l]

    def seg(name):
        lo, size = offs[name]
        return w_in[:, lo:lo + size]

    def zeros(k):
        return jnp.zeros((D, k), F32)

    half = MLA_ROPE // 2
    kr = seg('kr_d')
    kr_sw = jnp.concatenate([kr[:, half:], kr[:, :half]], axis=1)
    w_small = jnp.concatenate([
        seg('cq_d'), seg('ckv_d'), kr, kr_sw, zeros(LANES - 2 * MLA_ROPE), seg('f_b'), zeros(LANES - N_HEADS),
        seg('q_a'), seg('k_a'), seg('v_a'), seg('q_idx'), seg('k_idx'), seg('w_idx'),
        zeros(LANES - IDX_DIM - N_IDX_HEADS), seg('q_b'), seg('k_b'), seg('v_b'), seg('q_c'), seg('k_c'), seg('v_c'),
    ], axis=1).astype(BF16)
    assert w_small.shape[1] == Z_SMALL

    dq = MLA_NOPE + MLA_ROPE
    w_uq = W['w_uq'][l].reshape(MLA_Q_RANK, N_HEADS, dq)
    w_nope = w_uq[:, :, :MLA_NOPE].reshape(MLA_Q_RANK, N_HEADS * MLA_NOPE)
    rope_w = w_uq[:, :, MLA_NOPE:]
    rope_sw = jnp.concatenate([rope_w[..., half:], rope_w[..., :half]], axis=-1)

    def widen(r):
        wide = jnp.zeros((MLA_Q_RANK, N_HEADS, MLA_QW), F32).at[:, :, MLA_KV_RANK:MLA_KV_RANK + MLA_ROPE].set(r)
        return wide.reshape(MLA_Q_RANK, N_HEADS * MLA_QW)

    wq_all = jnp.concatenate([w_nope, widen(rope_w), widen(rope_sw)], axis=1).astype(BF16)
    w_ukv = W['w_ukv'][l]
    w_uk = jnp.transpose(w_ukv[..., :MLA_NOPE], (1, 2, 0))
    w_uk = jnp.pad(w_uk, ((0, 0), (0, 0), (0, MLA_QW - MLA_KV_RANK)))
    wuk_bd = jnp.einsum('hnc,hg->hngc', w_uk, jnp.eye(N_HEADS, dtype=F32))
    wuk_bd = wuk_bd.reshape(N_HEADS * MLA_NOPE, N_HEADS * MLA_QW).astype(BF16)
    w_uv = jnp.transpose(w_ukv[..., MLA_NOPE:], (1, 0, 2)).astype(BF16)

    return {
        'w_small': w_small, 'w_gates': seg('gates').astype(BF16),
        'wq_all': wq_all, 'wuk_bd': wuk_bd, 'w_uv': w_uv,
        'g_cq': W['g_cq'][l].reshape(1, -1), 'g_ckv': W['g_ckv'][l].reshape(1, -1),
        'b_forget': jnp.pad(W['b_forget'][l], (0, LANES - N_HEADS)).reshape(1, LANES),
        'w_branch': W['w_branch'][l].astype(BF16), 'w_out': W['w_out'][l].astype(BF16),
        'w_peer_q': W['w_peer_q'][l].astype(BF16),
        'peer_k1': W['peer_k1'][l].astype(BF16), 'peer_k2': W['peer_k2'][l].astype(BF16),
        'peer_u': W['peer_u'][l].astype(BF16), 'peer_v': W['peer_v'][l].astype(BF16),
    }


def _rope_tables(pos):
    half = MLA_ROPE // 2
    inv_freq = ROPE_THETA ** (-jnp.arange(half, dtype=F32) / half)
    ang = pos[:, None].astype(F32) * inv_freq
    cos, sin = jnp.cos(ang), jnp.sin(ang)
    pad = jnp.zeros((pos.shape[0], LANES - MLA_ROPE), F32)
    c128 = jnp.concatenate([cos, cos, pad], axis=1)
    s128 = jnp.concatenate([-sin, sin, pad], axis=1)
    lead = jnp.zeros((pos.shape[0], MLA_KV_RANK), F32)
    return c128, s128, jnp.concatenate([lead, c128], axis=1), jnp.concatenate([lead, s128], axis=1)


def _bias_tiles(t5):
    i = jnp.arange(TQ, dtype=jnp.int32)
    rel = jnp.arange(2, dtype=jnp.int32)[:, None, None] * TQ + i[None, :, None] - i[None, None, :]
    far = t5[N_BUCKETS - 1]
    tiles = jnp.where((rel >= 0)[..., None], t5[_rel_bucket(rel)] - far, 0.0)
    return jnp.transpose(tiles, (0, 3, 1, 2)), far.reshape(-1, 1, 1)


def _run_trunk(x, mods, past, W, LW, t5_table):
    B, T, D = x.shape
    n = B * T
    depth = len(LW)
    n_past = 0 if past is None else past[1].shape[1] * PAGE_SIZE
    rope = _rope_tables(n_past + jnp.arange(T, dtype=jnp.int32))
    if past is None:
        bias_a = _bias_tiles(t5_table[:, :N_HEADS])
        bias_c = _bias_tiles(t5_table[:, N_HEADS:])
    xf = x.reshape(n, D)
    delta = gate = None
    per_layer = []
    for l in range(depth):
        lw = LW[l]
        sh1, sc1, g1, sh2, sc2, g2 = jnp.split(mods[l], 6, axis=-1)
        xf, h = _norm(xf, T, W['norm_mix'][l], delta, gate, sc1, sh1)
        z = _mm(h, lw['w_small'])
        logf, d_lat, d_rope, q_mla, kc = _prep(z, T, lw, rope)
        rows = (z[:, Z_KVA:Z_KVA + 2 * HEAD_DIM].reshape(B, T, 2, HEAD_DIM),
                z[:, Z_KIDX:Z_KIDX + IDX_DIM].reshape(B, T, IDX_DIM),
                z[:, Z_KVB:Z_KVB + 2 * HEAD_DIM].reshape(B, T, 2, HEAD_DIM),
                logf[:, :N_HEADS].reshape(B, T, N_HEADS),
                z[:, Z_KVC:Z_KVC + 2 * HEAD_DIM].reshape(B, T, 2, HEAD_DIM),
                d_lat.reshape(B, T, MLA_KV_RANK),
                d_rope[:, :MLA_ROPE].reshape(B, T, MLA_ROPE))
        if past is None:
            o = _prompt_mixers(z, logf, q_mla, kc, lw, bias_a, bias_c, B, T)
        else:
            assert T == 1
            o = _paged_mixers(z, rows, logf, q_mla, l, lw, t5_table, past[0], past[1], B)
        mixed = _gate_mix(h, lw['w_gates'], o, lw['w_branch'])
        mix = _mm(mixed, lw['w_out'])
        xf, h2 = _norm(xf, T, W['norm_ffn'][l], mix, g1, sc2, sh2)
        delta, gate = _peer(h2, lw), g2
        per_layer.append(rows)
    _, y = _norm(xf, T, W['norm_final'], delta, gate, out_dtype=F32)
    stacked = tuple(jnp.stack([r[i] for r in per_layer], axis=0) for i in range(7))
    return y.reshape(B, T, D), stacked


def kernel(x_prompt, x_sample, cache_a_kv, cache_a_idx, cache_b_kv, cache_b_logf, cache_c_kv,
           cache_d_latent, cache_d_rope, page_table, c_prompt, c_sample, t5_table, w_ada, b_ada,
           norm_mix, norm_ffn, w_in, b_forget, g_cq, g_ckv, w_uq, w_ukv, w_branch, w_out,
           w_peer_q, peer_k1, peer_k2, peer_u, peer_v, norm_final):
    W = {
        'norm_mix': norm_mix, 'norm_ffn': norm_ffn, 'w_in': w_in, 'b_forget': b_forget, 'g_cq': g_cq,
        'g_ckv': g_ckv, 'w_uq': w_uq, 'w_ukv': w_ukv, 'w_branch': w_branch, 'w_out': w_out,
        'w_peer_q': w_peer_q, 'peer_k1': peer_k1, 'peer_k2': peer_k2, 'peer_u': peer_u,
        'peer_v': peer_v, 'norm_final': norm_final,
    }
    caches = {
        'a_kv': cache_a_kv, 'a_idx': cache_a_idx, 'b_kv': cache_b_kv, 'b_logf': cache_b_logf,
        'c_kv': cache_c_kv, 'd_lat': cache_d_latent, 'd_rope': cache_d_rope,
    }
    depth, D = w_in.shape[0], w_in.shape[1]
    n_prompt = c_prompt.shape[0]
    LW = [_layer_weights(l, W, D) for l in range(depth)]
    c_all = jnp.concatenate([c_prompt, c_sample], axis=0)
    mods = [_mm(c_all, w_ada[l], bias=b_ada[l], silu_in=True) for l in range(depth)]
    y_prompt, rows_p = _run_trunk(x_prompt, [m[:n_prompt] for m in mods], None, W, LW, t5_table)
    y_sample, rows_s = _run_trunk(x_sample, [m[n_prompt:] for m in mods], (caches, page_table), W, LW, t5_table)
    return (y_prompt, y_sample) + tuple(rows_p) + tuple(rows_s)
```

```python
import math
from functools import partial

import jax
import jax.numpy as jnp
from jax import lax
from jax.experimental import pallas as pl
from jax.experimental.pallas import tpu as pltpu

PAGE_SIZE = 128
HEAD_DIM = 64
N_HEADS = 8
N_IDX_HEADS = 16
IDX_DIM = 32
DSA_TOPK = 256
MOBA_BLOCK = 256
MOBA_TOPK = 3
MLA_Q_RANK = 384
MLA_KV_RANK = 128
MLA_NOPE = 64
MLA_ROPE = 32
MLA_V = 64
ROPE_THETA = 10000.0
N_BUCKETS = 32
MAX_DISTANCE = 128
N_BRANCH = 4
BRANCH_WIDTH = N_HEADS * HEAD_DIM
PEER_HEADS = 8
PEER_KEY_DIM = 256
PEER_N_KEYS = 128
PEER_TOPK = 16
PEER_SEL = PEER_HEADS * PEER_TOPK
EPS = 1e-6
NEG = -1e30

LANES = 128
VMEM_LIMIT = 56 * 1024 * 1024
TQ = 256
MLA_QW = 256
DSA_FOLD = 8
CAUSAL_GROUPS = 4

F32 = jnp.float32
BF16 = jnp.bfloat16

Z_CQ, Z_CKV, Z_KR, Z_FB = 0, 384, 512, 640
Z_PREP_W = 768
Z_QA, Z_KVA, Z_QIDX, Z_KIDX = 768, 1280, 1408, 1920
Z_WIDX = Z_KIDX + IDX_DIM
Z_QB, Z_KVB, Z_QC, Z_KVC = 2048, 2560, 2688, 3200
Z_SMALL = 3328


def _in_offsets(d_model):
    layout = (
        ('q_a', N_HEADS * HEAD_DIM), ('k_a', HEAD_DIM), ('v_a', HEAD_DIM),
        ('q_idx', N_IDX_HEADS * IDX_DIM), ('k_idx', IDX_DIM), ('w_idx', N_IDX_HEADS),
        ('q_b', N_HEADS * HEAD_DIM), ('k_b', HEAD_DIM), ('v_b', HEAD_DIM), ('f_b', N_HEADS),
        ('q_c', N_HEADS * HEAD_DIM), ('k_c', HEAD_DIM), ('v_c', HEAD_DIM),
        ('cq_d', MLA_Q_RANK), ('ckv_d', MLA_KV_RANK), ('kr_d', MLA_ROPE),
        ('gates', N_BRANCH * d_model),
    )
    offs, off = {}, 0
    for name, size in layout:
        offs[name] = (off, size)
        off += size
    return offs


def _cparams(*sem):
    return pltpu.CompilerParams(dimension_semantics=sem, vmem_limit_bytes=VMEM_LIMIT)


def _dot(a, b):
    return jnp.dot(a, b, preferred_element_type=F32)


def _dot_nt(a, b):
    return lax.dot_general(a, b, (((1,), (1,)), ((), ())), preferred_element_type=F32)


def _pick(n, cands):
    for c in cands:
        if n % c == 0:
            return c
    return n


def _mm_kernel(*refs, silu_in, has_bias):
    a_ref, w_ref = refs[0], refs[1]
    o_ref = refs[-1]
    a = a_ref[...]
    if silu_in:
        a = a.astype(F32)
        a = a * jax.nn.sigmoid(a)
    acc = _dot(a.astype(BF16), w_ref[...].astype(BF16))
    if has_bias:
        acc = acc + refs[2][...]
    o_ref[...] = acc.astype(o_ref.dtype)


def _mm(a, w, bias=None, silu_in=False, out_dtype=F32):
    M, K = a.shape
    N = w.shape[1]
    tm = _pick(M, (1024, 512, 256, 128))
    tn = _pick(N, (512, 256, 128))
    in_specs = [pl.BlockSpec((tm, K), lambda i, j: (i, 0)),
                pl.BlockSpec((K, tn), lambda i, j: (0, j))]
    args = [a, w]
    if bias is not None:
        in_specs.append(pl.BlockSpec((1, tn), lambda i, j: (0, j)))
        args.append(bias.reshape(1, N).astype(F32))
    return pl.pallas_call(
        partial(_mm_kernel, silu_in=silu_in, has_bias=bias is not None),
        grid=(M // tm, N // tn),
        in_specs=in_specs,
        out_specs=pl.BlockSpec((tm, tn), lambda i, j: (i, j)),
        out_shape=jax.ShapeDtypeStruct((M, N), out_dtype),
        compiler_params=_cparams("parallel", "parallel"),
    )(*args)


def _norm_kernel(*refs, has_delta, has_mod):
    it = iter(refs)
    x_ref = next(it)
    x = x_ref[...]
    if has_delta:
        d_ref, gt_ref = next(it), next(it)
        x = x + gt_ref[...] * d_ref[...]
    g_ref = next(it)
    if has_mod:
        sc_ref, sh_ref = next(it), next(it)
    if has_delta:
        xo_ref = next(it)
        xo_ref[...] = x
    h_ref = next(it)
    y = x * lax.rsqrt(jnp.mean(x * x, axis=-1, keepdims=True) + EPS) * g_ref[...]
    if has_mod:
        y = y * (1.0 + sc_ref[...]) + sh_ref[...]
    h_ref[...] = y.astype(h_ref.dtype)


def _norm(x, T, g, delta=None, gate=None, sc=None, sh=None, out_dtype=BF16):
    n, D = x.shape
    tr = _pick(n, (256, 128)) if T > 1 else n
    if T > 1:
        assert T % tr == 0

    def row_spec():
        return pl.BlockSpec((tr, D), lambda i: (i, 0))

    def mod_arg(a):
        if T == 1:
            return a, row_spec()
        return a[:, None, :], pl.BlockSpec((None, 1, D), lambda i: (i * tr // T, 0, 0))

    has_delta, has_mod = delta is not None, sc is not None
    args, specs = [x], [row_spec()]
    if has_delta:
        ga, gs = mod_arg(gate)
        args += [delta, ga]
        specs += [row_spec(), gs]
    args.append(g.reshape(1, D))
    specs.append(pl.BlockSpec((1, D), lambda i: (0, 0)))
    if has_mod:
        for a in (sc, sh):
            aa, ss = mod_arg(a)
            args.append(aa)
            specs.append(ss)
    out_shape, out_specs = [], []
    if has_delta:
        out_shape.append(jax.ShapeDtypeStruct((n, D), F32))
        out_specs.append(row_spec())
    out_shape.append(jax.ShapeDtypeStruct((n, D), out_dtype))
    out_specs.append(row_spec())
    outs = pl.pallas_call(
        partial(_norm_kernel, has_delta=has_delta, has_mod=has_mod),
        grid=(n // tr,), in_specs=specs, out_specs=out_specs, out_shape=out_shape,
        compiler_params=_cparams("parallel"),
    )(*args)
    if has_delta:
        return outs[0], outs[1]
    return x, outs[0]


def _prep_kernel(z_ref, gcq_ref, gckv_ref, bf_ref, wq_ref, wuk_ref, c128_ref, s128_ref, cq_ref, sq_ref,
                 logf_ref, dlat_ref, drope_ref, qm_ref, kc_ref):
    nope_w = N_HEADS * MLA_NOPE
    qw = N_HEADS * MLA_QW
    cq = z_ref[:, Z_CQ:Z_CQ + MLA_Q_RANK]
    ckv = z_ref[:, Z_CKV:Z_CKV + MLA_KV_RANK]
    kr = z_ref[:, Z_KR:Z_KR + LANES]
    fb = z_ref[:, Z_FB:Z_FB + LANES]

    def rms(x, g):
        return x * lax.rsqrt(jnp.mean(x * x, axis=-1, keepdims=True) + EPS) * g

    cqn = rms(cq, gcq_ref[...]).astype(BF16)
    qall = _dot(cqn, wq_ref[...])
    q_nope = qall[:, :nope_w].astype(BF16)
    q_r = qall[:, nope_w:nope_w + qw]
    q_rs = qall[:, nope_w + qw:]
    q_lat = _dot(q_nope, wuk_ref[...])
    cos_q = jnp.concatenate([cq_ref[...]] * N_HEADS, axis=1)
    sin_q = jnp.concatenate([sq_ref[...]] * N_HEADS, axis=1)
    scale = (MLA_NOPE + MLA_ROPE) ** -0.5
    qm_ref[...] = ((q_lat + q_r * cos_q + q_rs * sin_q) * scale).astype(BF16)

    dlat = rms(ckv, gckv_ref[...])
    dlat_ref[...] = dlat
    kr_sw = pltpu.roll(kr, LANES - MLA_ROPE, axis=1)
    drope = kr * c128_ref[...] + kr_sw * s128_ref[...]
    drope_ref[...] = drope
    kc_ref[...] = jnp.concatenate([dlat, drope], axis=1).astype(BF16)

    xf = fb + bf_ref[...]
    logf_ref[...] = jnp.minimum(xf, 0.0) - jnp.log1p(jnp.exp(-jnp.abs(xf)))


def _prep(z, T, lw, rope):
    n = z.shape[0]
    tm = _pick(n, (512, 256, 128))
    c128, s128, cq, sq = rope
    if T == 1:
        def tab_spec(w):
            return pl.BlockSpec((1, w), lambda i: (0, 0))
    else:
        assert T % tm == 0
        nt = T // tm

        def tab_spec(w):
            return pl.BlockSpec((tm, w), lambda i: (i % nt, 0))

    def full(a):
        return pl.BlockSpec(a.shape, lambda i: (0,) * a.ndim)

    def rows(w):
        return pl.BlockSpec((tm, w), lambda i: (i, 0))

    return pl.pallas_call(
        _prep_kernel,
        grid=(n // tm,),
        in_specs=[pl.BlockSpec((tm, Z_PREP_W), lambda i: (i, 0)),
                  full(lw['g_cq']), full(lw['g_ckv']), full(lw['b_forget']), full(lw['wq_all']),
                  full(lw['wuk_bd']), tab_spec(LANES), tab_spec(LANES), tab_spec(MLA_QW), tab_spec(MLA_QW)],
        out_specs=[rows(LANES), rows(LANES), rows(LANES), rows(N_HEADS * MLA_QW), rows(MLA_QW)],
        out_shape=[jax.ShapeDtypeStruct((n, LANES), F32), jax.ShapeDtypeStruct((n, LANES), F32),
                   jax.ShapeDtypeStruct((n, LANES), F32),
                   jax.ShapeDtypeStruct((n, N_HEADS * MLA_QW), BF16),
                   jax.ShapeDtypeStruct((n, MLA_QW), BF16)],
        compiler_params=_cparams("parallel"),
    )(z, lw['g_cq'], lw['g_ckv'], lw['b_forget'], lw['wq_all'], lw['wuk_bd'], c128, s128, cq, sq)


CUM_CHUNK = 256


def _cumsum_kernel(x_ref, o_ref):
    T = x_ref.shape[0]
    r = lax.broadcasted_iota(jnp.int32, (CUM_CHUNK, CUM_CHUNK), 0)
    c = lax.broadcasted_iota(jnp.int32, (CUM_CHUNK, CUM_CHUNK), 1)
    tri = jnp.where(r >= c, 1.0, 0.0).astype(BF16)

    def body(i, carry):
        off = pl.multiple_of(i * CUM_CHUNK, CUM_CHUNK)
        x = x_ref[pl.ds(off, CUM_CHUNK), :]
        hi = x.astype(BF16)
        r1 = x - hi.astype(F32)
        mid = r1.astype(BF16)
        lo = (r1 - mid.astype(F32)).astype(BF16)
        f = _dot(tri, hi) + _dot(tri, mid) + _dot(tri, lo) + carry
        o_ref[pl.ds(off, CUM_CHUNK), :] = f
        return f[CUM_CHUNK - 1:CUM_CHUNK, :]

    lax.fori_loop(0, T // CUM_CHUNK, body, jnp.zeros((1, LANES), F32))


def _cumsum(x):
    B, T, _ = x.shape
    assert T % CUM_CHUNK == 0
    spec = pl.BlockSpec((None, T, LANES), lambda b: (b, 0, 0))
    return pl.pallas_call(
        _cumsum_kernel, grid=(B,), in_specs=[spec], out_specs=spec,
        out_shape=jax.ShapeDtypeStruct(x.shape, F32), compiler_params=_cparams("parallel"),
    )(x)


def _softmax_pv(s, v):
    m = jnp.max(s, axis=-1, keepdims=True)
    p = jnp.exp(s - m)
    l = jnp.sum(p, axis=-1, keepdims=True)
    return _dot(p.astype(BF16), v) / l


def _causal(qi, T):
    row = lax.broadcasted_iota(jnp.int32, (TQ, T), 0) + qi * TQ
    col = lax.broadcasted_iota(jnp.int32, (TQ, T), 1)
    return row, col


def _fox_kernel(q_ref, k_ref, v_ref, fq_ref, fk_ref, o_ref, *, q0=0):
    qi = pl.program_id(1) + q0
    k, v = k_ref[...], v_ref[...]
    row, col = _causal(qi, k.shape[0])
    causal = col <= row

    def body(h, _):
        s = _dot_nt(q_ref[h], k) + fq_ref[h] - fk_ref[h]
        o_ref[h] = _softmax_pv(jnp.where(causal, s, NEG), v)
        return 0

    lax.fori_loop(0, N_HEADS, body, 0)


def _mla_kernel(q_ref, kc_ref, wuv_ref, o_ref, *, q0=0):
    qi = pl.program_id(1) + q0
    kc = kc_ref[...]
    v = kc[:, :MLA_KV_RANK]
    row, col = _causal(qi, kc.shape[0])
    causal = col <= row

    def body(h, _):
        s = _dot_nt(q_ref[h], kc)
        o_lat = _softmax_pv(jnp.where(causal, s, NEG), v)
        o_ref[h] = _dot(o_lat.astype(BF16), wuv_ref[h])
        return 0

    lax.fori_loop(0, N_HEADS, body, 0)


def _biased_logits(q, k, h, qi, dt_ref, c_ref, s_ref):
    off = pl.multiple_of(qi * TQ, TQ)
    s_ref[...] = _dot_nt(q, k) + c_ref[h]
    s_ref[:, pl.ds(off, TQ)] += dt_ref[0, h]

    @pl.when(qi > 0)
    def _():
        s_ref[:, pl.ds(off - TQ, TQ)] += dt_ref[1, h]

    return s_ref[...]


def _moba_kernel(q_ref, k_ref, v_ref, kf_ref, dt_ref, c_ref, o_ref, s_ref, *, k_sel):
    qi = pl.program_id(1)
    k, v = k_ref[...], v_ref[...]
    T = k.shape[0]
    nb = T // MOBA_BLOCK
    km = jnp.mean(kf_ref[...].reshape(nb, MOBA_BLOCK, HEAD_DIM), axis=1)
    km = jnp.concatenate([km, jnp.zeros((LANES - nb, HEAD_DIM), F32)], axis=0).astype(BF16)
    row, col = _causal(qi, T)
    own = (col >= qi * TQ) & (col <= row)
    lane = lax.broadcasted_iota(jnp.int32, (TQ, LANES), 1)
    eb = lax.broadcasted_iota(jnp.int32, (LANES, T), 0)
    es = lax.broadcasted_iota(jnp.int32, (LANES, T), 1)
    expand = jnp.where(es // MOBA_BLOCK == eb, 1.0, 0.0).astype(BF16)

    def body(h, _):
        q = q_ref[h]
        gate = _dot_nt(q, km)
        rank = jnp.zeros((TQ, LANES), F32)
        for m in range(nb):
            gm = gate[:, m:m + 1]
            beats = (gm > gate) | ((gm == gate) & (lane > m))
            rank = rank + jnp.where(beats, jnp.where(m < qi, 1.0, 0.0), 0.0)
        selb = jnp.where((rank < k_sel) & (lane < qi), 1.0, 0.0).astype(BF16)
        picked = _dot(selb, expand) > 0.5
        s = _biased_logits(q, k, h, qi, dt_ref, c_ref, s_ref)
        o_ref[h] = _softmax_pv(jnp.where(picked | own, s, NEG), v)
        return 0

    lax.fori_loop(0, N_HEADS, body, 0)


def _sortable(x):
    b = pltpu.bitcast(x, jnp.int32)
    return b ^ ((b >> 31) & jnp.int32(0x7FFFFFFF))


def _topk_mask(score, col, k_sel, n_idx_bits, tail=None, flat=False):
    key = _sortable(score)
    tkey = None if tail is None else _sortable(tail)
    kf = float(k_sel)

    def _count(mask):
        n = jnp.sum(jnp.where(mask, 1.0, 0.0), axis=-1, keepdims=True)
        return jnp.sum(n, axis=0, keepdims=True) if flat else n

    def count_ge(c):
        n = _count(key >= c)
        return n if tkey is None else n + jnp.where(tkey >= c, 1.0, 0.0)

    prefix = jnp.where(count_ge(jnp.int32(0)) >= kf, jnp.int32(0), jnp.int32(-2 ** 31))

    def vbody(i, prefix):
        cand = prefix | (jnp.int32(1) << (30 - i))
        return jnp.where(count_ge(cand) >= kf, cand, prefix)

    thr = lax.fori_loop(0, 31, vbody, prefix)
    above = key > thr
    tie = key == thr
    n_above = _count(above)
    if tkey is not None:
        n_above = n_above + jnp.where(tkey > thr, 1.0, 0.0)
    need = kf - n_above

    def ibody(i, p):
        cand = p | (jnp.int32(1) << (n_idx_bits - 1 - i))
        return jnp.where(_count(tie & (col < cand)) < need, cand, p)

    last = lax.fori_loop(0, n_idx_bits, ibody, jnp.zeros_like(thr))
    mask = above | (tie & (col <= last))
    if tkey is None:
        return mask
    return mask, (tkey > thr) | ((tkey == thr) & (_count(tie) < need))


def _dsa_kernel(q_ref, qi_ref, w_ref, kidx_ref, k_ref, v_ref, dt_ref, c_ref, o_ref, s_ref, sc_ref, *, k_sel, q0=0):
    qi = pl.program_id(1) + q0
    k, v, kidx = k_ref[...], v_ref[...], kidx_ref[...]
    T = k.shape[0]
    row, col = _causal(qi, T)
    causal = col <= row

    sc_ref[...] = jnp.zeros_like(sc_ref)

    def ibody(h, _):
        sc_ref[...] += w_ref[h] * jnp.maximum(_dot_nt(qi_ref[h], kidx), 0.0)
        return 0

    lax.fori_loop(0, N_IDX_HEADS, ibody, 0)
    score = jnp.where(causal, sc_ref[...], NEG)
    mask = _topk_mask(score, col, k_sel, max(1, (T - 1).bit_length())) & causal

    def body(h, _):
        s = _biased_logits(q_ref[h], k, h, qi, dt_ref, c_ref, s_ref)
        o_ref[h] = _softmax_pv(jnp.where(mask, s, NEG), v)
        return 0

    lax.fori_loop(0, N_HEADS, body, 0)


def _heads_spec(nh, w, q0=0):
    return pl.BlockSpec((None, nh, TQ, w), lambda b, i: (b, 0, i + q0, 0))


def _ctx_spec(T, w):
    return pl.BlockSpec((None, T, w), lambda b, i: (b, 0, 0))


def _full_spec(a):
    return pl.BlockSpec(a.shape, lambda b, i: (0,) * a.ndim)


def _attn_call(kern, B, T, in_specs, args, scratch=()):
    return pl.pallas_call(
        kern, grid=(B, T // TQ), in_specs=in_specs,
        out_specs=_heads_spec(N_HEADS, HEAD_DIM),
        out_shape=jax.ShapeDtypeStruct((B, N_HEADS, T, HEAD_DIM), F32),
        scratch_shapes=list(scratch),
        compiler_params=_cparams("parallel", "arbitrary"),
    )(*args)


def _to_heads(x, B, T, nh, d, scale=1.0):
    return (x.reshape(B, T, nh, d) * scale).astype(BF16).transpose(0, 2, 1, 3)


def _from_heads(o):
    B, nh, T, d = o.shape
    return o.transpose(0, 2, 1, 3).reshape(B * T, nh * d).astype(BF16)


def _prompt_mixers(z, logf, q_mla, kc, lw, bias_a, bias_c, B, T):
    assert T % TQ == 0 and TQ == MOBA_BLOCK and MAX_DISTANCE <= TQ
    scale = HEAD_DIM ** -0.5

    def ctx(lo, w):
        return z[:, lo:lo + w].reshape(B, T, w)

    sblock = pltpu.VMEM((TQ, T), F32)
    n_tiles = T // TQ
    group = n_tiles // CAUSAL_GROUPS if n_tiles % CAUSAL_GROUPS == 0 else n_tiles

    def grouped(call):
        return jnp.concatenate([call(q0, (q0 + group) * TQ) for q0 in range(0, n_tiles, group)], axis=2)

    q_a = _to_heads(z[:, Z_QA:Z_QA + BRANCH_WIDTH], B, T, N_HEADS, HEAD_DIM, scale)
    q_idx = _to_heads(z[:, Z_QIDX:Z_QIDX + N_IDX_HEADS * IDX_DIM], B, T, N_IDX_HEADS, IDX_DIM)
    w_idx = (z[:, Z_WIDX:Z_WIDX + N_IDX_HEADS] * N_IDX_HEADS ** -0.5).reshape(B, T, N_IDX_HEADS)
    w_idx = w_idx.transpose(0, 2, 1)[..., None]
    k_idx = ctx(Z_KIDX, IDX_DIM).astype(BF16)
    ka = ctx(Z_KVA, HEAD_DIM).astype(BF16)
    va = ctx(Z_KVA + HEAD_DIM, HEAD_DIM).astype(BF16)
    dt_a, c_a = bias_a
    o_a = grouped(lambda q0, ext: _attn_call(
        partial(_dsa_kernel, k_sel=min(DSA_TOPK, T // 4), q0=q0), B, group * TQ,
        [_heads_spec(N_HEADS, HEAD_DIM, q0), _heads_spec(N_IDX_HEADS, IDX_DIM, q0), _heads_spec(N_IDX_HEADS, 1, q0),
         _ctx_spec(ext, IDX_DIM), _ctx_spec(ext, HEAD_DIM), _ctx_spec(ext, HEAD_DIM), _full_spec(dt_a),
         _full_spec(c_a)],
        (q_a, q_idx, w_idx, k_idx, ka, va, dt_a, c_a), scratch=(pltpu.VMEM((TQ, ext), F32),) * 2))
    q_b = _to_heads(z[:, Z_QB:Z_QB + BRANCH_WIDTH], B, T, N_HEADS, HEAD_DIM, scale)
    kb = ctx(Z_KVB, HEAD_DIM).astype(BF16)
    vb = ctx(Z_KVB + HEAD_DIM, HEAD_DIM).astype(BF16)
    f_cum = _cumsum(logf.reshape(B, T, LANES))[:, :, :N_HEADS].transpose(0, 2, 1)
    o_b = grouped(lambda q0, ext: _attn_call(
        partial(_fox_kernel, q0=q0), B, group * TQ,
        [_heads_spec(N_HEADS, HEAD_DIM, q0), _ctx_spec(ext, HEAD_DIM), _ctx_spec(ext, HEAD_DIM),
         _heads_spec(N_HEADS, 1, q0), pl.BlockSpec((None, N_HEADS, 1, ext), lambda b, i: (b, 0, 0, 0))],
        (q_b, kb, vb, f_cum[..., None], f_cum[:, :, None, :])))
    q_c = _to_heads(z[:, Z_QC:Z_QC + BRANCH_WIDTH], B, T, N_HEADS, HEAD_DIM, scale)
    kc_f = ctx(Z_KVC, HEAD_DIM)
    vc = ctx(Z_KVC + HEAD_DIM, HEAD_DIM).astype(BF16)
    dt_c, c_c = bias_c
    o_c = _attn_call(
        partial(_moba_kernel, k_sel=min(MOBA_TOPK, T // MOBA_BLOCK)), B, T,
        [_heads_spec(N_HEADS, HEAD_DIM), _ctx_spec(T, HEAD_DIM), _ctx_spec(T, HEAD_DIM), _ctx_spec(T, HEAD_DIM),
         _full_spec(dt_c), _full_spec(c_c)],
        (q_c, kc_f.astype(BF16), vc, kc_f, dt_c, c_c), scratch=(sblock,))
    q_d = q_mla.reshape(B, T, N_HEADS, MLA_QW).transpose(0, 2, 1, 3)
    o_d = grouped(lambda q0, ext: _attn_call(
        partial(_mla_kernel, q0=q0), B, group * TQ,
        [_heads_spec(N_HEADS, MLA_QW, q0), _ctx_spec(ext, MLA_QW), _full_spec(lw['w_uv'])],
        (q_d, kc.reshape(B, T, MLA_QW), lw['w_uv'])))
    return jnp.stack([_from_heads(o) for o in (o_a, o_b, o_c, o_d)], axis=0)


def _gate_mix_kernel(h_ref, g0_ref, g1_ref, g2_ref, g3_ref, o_ref, wb_ref, out_ref):
    h = h_ref[...]
    acc = None
    for n, g_ref in enumerate((g0_ref, g1_ref, g2_ref, g3_ref)):
        gate = jax.nn.sigmoid(_dot(h, g_ref[...]))
        term = gate * _dot(o_ref[n], wb_ref[n])
        acc = term if acc is None else acc + term
    out_ref[...] = acc.astype(out_ref.dtype)


def _gate_mix(h, w_gates, o, w_branch):
    n, D = h.shape
    tm = _pick(n, (1024, 512, 256, 128))
    td = _pick(D, (256, 128))
    nd = D // td
    gate_specs = [pl.BlockSpec((D, td), partial(lambda i, j, b: (0, b * nd + j), b=b)) for b in range(N_BRANCH)]
    return pl.pallas_call(
        _gate_mix_kernel,
        grid=(n // tm, nd),
        in_specs=[pl.BlockSpec((tm, D), lambda i, j: (i, 0))] + gate_specs + [
            pl.BlockSpec((N_BRANCH, tm, BRANCH_WIDTH), lambda i, j: (0, i, 0)),
            pl.BlockSpec((N_BRANCH, BRANCH_WIDTH, td), lambda i, j: (0, 0, j))],
        out_specs=pl.BlockSpec((tm, td), lambda i, j: (i, j)),
        out_shape=jax.ShapeDtypeStruct((n, D), BF16),
        compiler_params=_cparams("parallel", "parallel"),
    )(h, w_gates, w_gates, w_gates, w_gates, o, w_branch)


def _top16(s):
    R, tn = s.shape
    rid = lax.broadcasted_iota(jnp.int32, (R, tn), 0)
    kid = lax.broadcasted_iota(jnp.int32, (PEER_TOPK, tn), 0)

    def body(k, carry):
        s, vals, idxs = carry
        m = jnp.max(s, axis=0, keepdims=True)
        i = jnp.min(jnp.where(s == m, rid, R), axis=0, keepdims=True)
        vals = jnp.where(kid == k, m, vals)
        idxs = jnp.where(kid == k, i, idxs)
        return jnp.where(rid == i, -jnp.inf, s), vals, idxs

    _, vals, idxs = lax.fori_loop(
        0, PEER_TOPK, body, (s, jnp.zeros((PEER_TOPK, tn), F32), jnp.zeros((PEER_TOPK, tn), jnp.int32)))
    return vals, idxs


def _peer_select_kernel(q_ref, k1_ref, k2_ref, e1_ref, e2_ref, g_ref):
    half = PEER_KEY_DIM // 2
    k1, k2 = k1_ref[...], k2_ref[...]

    def pick(idx, table):
        out = jnp.zeros_like(idx)
        for j in range(PEER_TOPK):
            out = out + jnp.where(idx == j, table[j:j + 1, :], 0)
        return out

    def body(h, _):
        off = pl.multiple_of(h * PEER_KEY_DIM, PEER_KEY_DIM)
        q1 = q_ref[:, pl.ds(off, half)].astype(BF16)
        q2 = q_ref[:, pl.ds(off + half, half)].astype(BF16)
        v1, i1 = _top16(_dot_nt(k1, q1))
        v2, i2 = _top16(_dot_nt(k2, q2))
        cand = jnp.concatenate([v1[a:a + 1, :] + v2 for a in range(PEER_TOPK)], axis=0)
        top, ci = _top16(cand)
        e = jnp.exp(top - jnp.max(top, axis=0, keepdims=True))
        rows = pl.ds(pl.multiple_of(h * PEER_TOPK, PEER_TOPK), PEER_TOPK)
        e1_ref[rows, :] = pick(ci // PEER_TOPK, i1)
        e2_ref[rows, :] = pick(ci % PEER_TOPK, i2)
        g_ref[rows, :] = e / jnp.sum(e, axis=0, keepdims=True)
        return 0

    lax.fori_loop(0, PEER_HEADS, body, 0)


def _peer_select(q, k1, k2):
    n = q.shape[0]
    tn = _pick(n, (256, 128))
    out_spec = pl.BlockSpec((PEER_SEL, tn), lambda i: (0, i))
    kspec = pl.BlockSpec(k1.shape, lambda i: (0, 0))
    return pl.pallas_call(
        _peer_select_kernel, grid=(n // tn,),
        in_specs=[pl.BlockSpec((tn, PEER_HEADS * PEER_KEY_DIM), lambda i: (i, 0)), kspec, kspec],
        out_specs=[out_spec] * 3,
        out_shape=[jax.ShapeDtypeStruct((PEER_SEL, n), jnp.int32)] * 2 + [jax.ShapeDtypeStruct((PEER_SEL, n), F32)],
        compiler_params=_cparams("parallel"),
    )(q, k1, k2)


PEER_G_TOKENS = 32


def _peer_gates_kernel(e1_ref, e2_ref, g_ref, o_ref):
    shape = (PEER_G_TOKENS, PEER_N_KEYS, PEER_SEL)
    rid = lax.broadcasted_iota(jnp.int32, shape, 1)
    p1 = jnp.where(e1_ref[...][:, None, :] == rid, 1.0, 0.0).astype(BF16)
    gp2 = jnp.where(e2_ref[...][:, None, :] == rid, g_ref[...][:, None, :], 0.0).astype(BF16)
    o_ref[...] = jnp.einsum('nrk,nck->nrc', p1, gp2, preferred_element_type=F32).astype(o_ref.dtype)


def _peer_gates(e1, e2, g):
    n = e1.shape[0]
    spec = pl.BlockSpec((PEER_G_TOKENS, PEER_SEL), lambda i: (i, 0))
    return pl.pallas_call(
        _peer_gates_kernel, grid=(n // PEER_G_TOKENS,),
        in_specs=[spec, spec, spec],
        out_specs=pl.BlockSpec((PEER_G_TOKENS, PEER_N_KEYS, PEER_N_KEYS), lambda i: (i, 0, 0)),
        out_shape=jax.ShapeDtypeStruct((n, PEER_N_KEYS, PEER_N_KEYS), BF16),
        compiler_params=_cparams("parallel"),
    )(e1, e2, g)


def _peer_dense_kernel(h_ref, u_ref, g_ref, v_ref, o_ref):
    @pl.when(pl.program_id(1) == 0)
    def _():
        o_ref[...] = jnp.zeros_like(o_ref)

    a = _dot_nt(h_ref[...], u_ref[...])
    act = 0.5 * a * (1.0 + lax.erf(a * (0.5 ** 0.5)))
    w = (g_ref[...].astype(F32) * act).astype(BF16)
    o_ref[...] += _dot(w, v_ref[...])


def _peer_dense(h, u, g, v):
    n, D = h.shape
    E = u.shape[0]
    tn = _pick(n, (512, 256, 128))
    te = _pick(E, (512, 256, 128))
    return pl.pallas_call(
        _peer_dense_kernel, grid=(n // tn, E // te),
        in_specs=[pl.BlockSpec((tn, D), lambda i, j: (i, 0)), pl.BlockSpec((te, D), lambda i, j: (j, 0)),
                  pl.BlockSpec((tn, te), lambda i, j: (i, j)), pl.BlockSpec((te, D), lambda i, j: (j, 0))],
        out_specs=pl.BlockSpec((tn, D), lambda i, j: (i, 0)),
        out_shape=jax.ShapeDtypeStruct((n, D), F32),
        compiler_params=_cparams("parallel", "arbitrary"),
    )(h, u, g, v)


def _peer(h, lw):
    q = _mm(h, lw['w_peer_q'])
    e1, e2, g = _peer_select(q, lw['peer_k1'], lw['peer_k2'])
    grid = _peer_gates(e1.T, e2.T, g.T)
    return _peer_dense(h, lw['peer_u'], grid.reshape(h.shape[0], PEER_N_KEYS * PEER_N_KEYS), lw['peer_v'])


def _rel_bucket(rel):
    n_exact = N_BUCKETS // 2
    relf = jnp.maximum(rel, 1).astype(F32)
    large = n_exact + (jnp.log(relf / n_exact) / math.log(MAX_DISTANCE / n_exact)
                       * (N_BUCKETS - n_exact)).astype(jnp.int32)
    large = jnp.minimum(large, N_BUCKETS - 1)
    return jnp.where(rel < n_exact, jnp.maximum(rel, 0), large)


def _dec_softmax(s, s_new, vmat, v_new, transposed=True):
    m = jnp.maximum(jnp.max(s, axis=-1, keepdims=True), s_new)
    p = jnp.exp(s - m)
    pn = jnp.exp(s_new - m)
    l = jnp.sum(p, axis=-1, keepdims=True) + pn
    pv = _dot_nt(p.astype(BF16), vmat) if transposed else _dot(p.astype(BF16), vmat)
    return (pv + pn * v_new) / l


def _new_logit(q, new_row):
    return jnp.sum(q.astype(F32) * new_row.astype(BF16).astype(F32), axis=-1, keepdims=True)


def _pages_t(ref):
    return jnp.concatenate([ref[p] for p in range(ref.shape[0])], axis=1)


def _fox_dec_kernel(q_ref, kv_ref, new_ref, fk_ref, ln_ref, o_ref):
    q, new = q_ref[...].astype(BF16), new_ref[...]
    kvt = _pages_t(kv_ref).astype(BF16)
    fk = fk_ref[...]
    f_t = fk[:, fk.shape[1] - 1:] + ln_ref[...]
    s = _dot(q, kvt) + (f_t - fk)
    o = _dec_softmax(s, _new_logit(q, new), kvt, new)
    o_ref[...] = o[:, HEAD_DIM:]


def _dsa_dec_kernel(q_ref, qi_ref, w_ref, kidx_ref, kinew_ref, kv_ref, new_ref, bias_ref, bnew_ref, o_ref, *, k_sel):
    q, new, w = q_ref[...].astype(BF16), new_ref[...], w_ref[...]
    qidx = qi_ref[...].astype(BF16)
    dots = jnp.maximum(_dot(qidx, _pages_t(kidx_ref).astype(BF16)), 0.0)
    score = jnp.sum(w * dots, axis=0, keepdims=True)
    score_new = jnp.sum(w * jnp.maximum(_new_logit(qidx, kinew_ref[...]), 0.0), axis=0, keepdims=True)
    Lp = score.shape[1]
    chunk = Lp // DSA_FOLD
    folded = jnp.concatenate([score[:, r * chunk:(r + 1) * chunk] for r in range(DSA_FOLD)], axis=0)
    col = (lax.broadcasted_iota(jnp.int32, folded.shape, 0) * chunk
           + lax.broadcasted_iota(jnp.int32, folded.shape, 1))
    picked, mask_new = _topk_mask(folded, col, k_sel, max(1, (Lp - 1).bit_length()), tail=score_new, flat=True)
    picked = jnp.where(picked, 1.0, 0.0)
    mask = jnp.concatenate([picked[r:r + 1, :] for r in range(DSA_FOLD)], axis=1) > 0.5
    kvt = _pages_t(kv_ref).astype(BF16)
    s = jnp.where(mask, _dot(q, kvt) + bias_ref[...], NEG)
    s_new = jnp.where(mask_new, _new_logit(q, new) + bnew_ref[...], NEG)
    o_ref[...] = _dec_softmax(s, s_new, kvt, new)[:, HEAD_DIM:]


def _moba_dec_kernel(q_ref, kv_ref, new_ref, exp_ref, bias_ref, bnew_ref, o_ref, *, k_sel):
    q, new = q_ref[...].astype(BF16), new_ref[...]
    kv = _pages_t(kv_ref)
    nb = kv.shape[1] // MOBA_BLOCK
    expand = exp_ref[...]
    kvt = kv.astype(BF16)
    rest = (kv - kvt.astype(F32)).astype(BF16)
    km = (_dot_nt(kvt, expand) + _dot_nt(rest, expand)) * (1.0 / MOBA_BLOCK)
    gate = _dot(q, km.astype(BF16))
    lane = lax.broadcasted_iota(jnp.int32, gate.shape, 1)
    rank = jnp.zeros(gate.shape, F32)
    for m in range(nb):
        gm = gate[:, m:m + 1]
        rank = rank + jnp.where((gm > gate) | ((gm == gate) & (lane > m)), 1.0, 0.0)
    selb = jnp.where((rank < k_sel) & (lane < nb), 1.0, 0.0).astype(BF16)
    picked = _dot(selb, expand) > 0.5
    s = jnp.where(picked, _dot(q, kvt) + bias_ref[...], NEG)
    o_ref[...] = _dec_softmax(s, _new_logit(q, new) + bnew_ref[...], kvt, new)[:, HEAD_DIM:]


def _mla_dec_kernel(ql_ref, qr_ref, lat_ref, rope_ref, lnew_ref, rnew_ref, o_ref):
    ql, qr = ql_ref[...].astype(BF16), qr_ref[...].astype(BF16)
    latb = lat_ref[...].astype(BF16)
    s = _dot_nt(ql, latb) + _dot(qr, _pages_t(rope_ref).astype(BF16))
    s_new = _new_logit(ql, lnew_ref[...]) + _new_logit(qr, rnew_ref[...])
    o_ref[...] = _dec_softmax(s, s_new, latb, lnew_ref[...], transposed=False)


def _head_proj_kernel(x_ref, w_ref, o_ref):
    o_ref[...] = _dot(x_ref[...].astype(BF16), w_ref[...])


def _head_proj(x, w):
    H, B, c = x.shape
    n = w.shape[2]
    return pl.pallas_call(
        _head_proj_kernel, grid=(H,),
        in_specs=[pl.BlockSpec((None, B, c), lambda h: (h, 0, 0)), pl.BlockSpec((None, c, n), lambda h: (h, 0, 0))],
        out_specs=pl.BlockSpec((None, B, n), lambda h: (h, 0, 0)),
        out_shape=jax.ShapeDtypeStruct((H, B, n), F32),
        compiler_params=_cparams("parallel"),
    )(x, w)


def _dec_call(kern, B, in_specs, args, width=HEAD_DIM):
    return pl.pallas_call(
        kern, grid=(B,), in_specs=in_specs,
        out_specs=pl.BlockSpec((None, N_HEADS, width), lambda b: (b, 0, 0)),
        out_shape=jax.ShapeDtypeStruct((B, N_HEADS, width), F32),
        compiler_params=_cparams("parallel"),
    )(*args)


def _per_seq(a):
    return pl.BlockSpec((None,) + a.shape[1:], lambda b: (b,) + (0,) * (a.ndim - 1))


def _shared(a):
    return pl.BlockSpec(a.shape, lambda b: (0,) * a.ndim)


def _paged_mixers(z, rows, logf, q_mla, l, lw, t5_table, caches, page_table, B):
    n_pages = page_table.shape[1]
    Lp = n_pages * PAGE_SIZE
    assert Lp % MOBA_BLOCK == 0 and Lp // MOBA_BLOCK <= LANES and MAX_DISTANCE <= Lp
    scale = HEAD_DIM ** -0.5

    def past_t(name, w):
        c = caches[name]
        pool = jnp.moveaxis(c, 2, -1).reshape(c.shape[0] * c.shape[1], w, PAGE_SIZE)
        return pool[page_table + l * c.shape[1]]

    def past_rows(name, w):
        c = caches[name]
        return c.reshape(c.shape[0] * c.shape[1], PAGE_SIZE, w)[page_table + l * c.shape[1]].reshape(B, Lp, w)

    def q_pad(lo):
        q = z[:, lo:lo + BRANCH_WIDTH].reshape(B, N_HEADS, HEAD_DIM) * scale
        return jnp.pad(q, ((0, 0), (0, 0), (0, HEAD_DIM)))

    def new_row(lo, w):
        return z[:, lo:lo + w].reshape(B, 1, w)

    def bias_rows(t5):
        rel = Lp - jnp.arange(Lp, dtype=jnp.int32)
        return t5[_rel_bucket(rel)].T, t5[0].reshape(-1, 1)

    kv_w = 2 * HEAD_DIM
    q_a = q_pad(Z_QA)
    q_idx = z[:, Z_QIDX:Z_QIDX + N_IDX_HEADS * IDX_DIM].reshape(B, N_IDX_HEADS, IDX_DIM)
    w_idx = (z[:, Z_WIDX:Z_WIDX + N_IDX_HEADS] * N_IDX_HEADS ** -0.5).reshape(B, N_IDX_HEADS, 1)
    args = (q_a, q_idx, w_idx, past_t('a_idx', IDX_DIM), new_row(Z_KIDX, IDX_DIM), past_t('a_kv', kv_w),
            new_row(Z_KVA, kv_w)) + bias_rows(t5_table[:, :N_HEADS])
    o_a = _dec_call(partial(_dsa_dec_kernel, k_sel=min(DSA_TOPK, (Lp + 1) // 4)), B,
                    [_per_seq(a) for a in args[:7]] + [_shared(a) for a in args[7:]], args)
    lf = past_t('b_logf', N_HEADS).transpose(1, 3, 0, 2).reshape(Lp, B * N_HEADS // LANES, LANES).transpose(1, 0, 2)
    f_past = _cumsum(lf).transpose(1, 0, 2).reshape(Lp, B, N_HEADS).transpose(1, 2, 0)
    args = (q_pad(Z_QB), past_t('b_kv', kv_w), new_row(Z_KVB, kv_w), f_past, logf[:, :N_HEADS].reshape(B, N_HEADS, 1))
    o_b = _dec_call(_fox_dec_kernel, B, [_per_seq(a) for a in args], args)
    blk = jnp.arange(LANES, dtype=jnp.int32)[:, None]
    expand = (jnp.arange(Lp, dtype=jnp.int32)[None, :] // MOBA_BLOCK == blk).astype(BF16)
    args = (q_pad(Z_QC), past_t('c_kv', kv_w), new_row(Z_KVC, kv_w), expand) + bias_rows(t5_table[:, N_HEADS:])
    o_c = _dec_call(partial(_moba_dec_kernel, k_sel=min(MOBA_TOPK, Lp // MOBA_BLOCK + 1)), B,
                    [_per_seq(a) for a in args[:3]] + [_shared(a) for a in args[3:]], args)
    qm = q_mla.reshape(B, N_HEADS, MLA_QW).astype(F32)
    d_lat, d_rope = rows[5], rows[6]
    args = (qm[..., :MLA_KV_RANK], qm[..., MLA_KV_RANK:MLA_KV_RANK + MLA_ROPE], past_rows('d_lat', MLA_KV_RANK),
            past_t('d_rope', MLA_ROPE), d_lat.reshape(B, 1, MLA_KV_RANK), d_rope.reshape(B, 1, MLA_ROPE))
    o_lat = _dec_call(_mla_dec_kernel, B, [_per_seq(a) for a in args], args, width=MLA_KV_RANK)
    o_d = _head_proj(o_lat.transpose(1, 0, 2), lw['w_uv']).transpose(1, 0, 2)
    return jnp.stack([o.reshape(B, BRANCH_WIDTH).astype(BF16) for o in (o_a, o_b, o_c, o_d)], axis=0)


def _layer_weights(l, W, D):
    offs = _in_offsets(D)
    w_in = W['w_in'][l]

    def seg(name):
        lo, size = offs[name]
        return w_in[:, lo:lo + size]

    def zeros(k):
        return jnp.zeros((D, k), F32)

    half = MLA_ROPE // 2
    kr = seg('kr_d')
    kr_sw = jnp.concatenate([kr[:, half:], kr[:, :half]], axis=1)
    w_small = jnp.concatenate([
        seg('cq_d'), seg('ckv_d'), kr, kr_sw, zeros(LANES - 2 * MLA_ROPE), seg('f_b'), zeros(LANES - N_HEADS),
        seg('q_a'), seg('k_a'), seg('v_a'), seg('q_idx'), seg('k_idx'), seg('w_idx'),
        zeros(LANES - IDX_DIM - N_IDX_HEADS), seg('q_b'), seg('k_b'), seg('v_b'), seg('q_c'), seg('k_c'), seg('v_c'),
    ], axis=1).astype(BF16)
    assert w_small.shape[1] == Z_SMALL

    dq = MLA_NOPE + MLA_ROPE
    w_uq = W['w_uq'][l].reshape(MLA_Q_RANK, N_HEADS, dq)
    w_nope = w_uq[:, :, :MLA_NOPE].reshape(MLA_Q_RANK, N_HEADS * MLA_NOPE)
    rope_w = w_uq[:, :, MLA_NOPE:]
    rope_sw = jnp.concatenate([rope_w[..., half:], rope_w[..., :half]], axis=-1)

    def widen(r):
        wide = jnp.zeros((MLA_Q_RANK, N_HEADS, MLA_QW), F32).at[:, :, MLA_KV_RANK:MLA_KV_RANK + MLA_ROPE].set(r)
        return wide.reshape(MLA_Q_RANK, N_HEADS * MLA_QW)

    wq_all = jnp.concatenate([w_nope, widen(rope_w), widen(rope_sw)], axis=1).astype(BF16)
    w_ukv = W['w_ukv'][l]
    w_uk = jnp.transpose(w_ukv[..., :MLA_NOPE], (1, 2, 0))
    w_uk = jnp.pad(w_uk, ((0, 0), (0, 0), (0, MLA_QW - MLA_KV_RANK)))
    wuk_bd = jnp.einsum('hnc,hg->hngc', w_uk, jnp.eye(N_HEADS, dtype=F32))
    wuk_bd = wuk_bd.reshape(N_HEADS * MLA_NOPE, N_HEADS * MLA_QW).astype(BF16)
    w_uv = jnp.transpose(w_ukv[..., MLA_NOPE:], (1, 0, 2)).astype(BF16)

    return {
        'w_small': w_small, 'w_gates': seg('gates').astype(BF16),
        'wq_all': wq_all, 'wuk_bd': wuk_bd, 'w_uv': w_uv,
        'g_cq': W['g_cq'][l].reshape(1, -1), 'g_ckv': W['g_ckv'][l].reshape(1, -1),
        'b_forget': jnp.pad(W['b_forget'][l], (0, LANES - N_HEADS)).reshape(1, LANES),
        'w_branch': W['w_branch'][l].astype(BF16), 'w_out': W['w_out'][l].astype(BF16),
        'w_peer_q': W['w_peer_q'][l].astype(BF16),
        'peer_k1': W['peer_k1'][l].astype(BF16), 'peer_k2': W['peer_k2'][l].astype(BF16),
        'peer_u': W['peer_u'][l].astype(BF16), 'peer_v': W['peer_v'][l].astype(BF16),
    }


def _rope_tables(pos):
    half = MLA_ROPE // 2
    inv_freq = ROPE_THETA ** (-jnp.arange(half, dtype=F32) / half)
    ang = pos[:, None].astype(F32) * inv_freq
    cos, sin = jnp.cos(ang), jnp.sin(ang)
    pad = jnp.zeros((pos.shape[0], LANES - MLA_ROPE), F32)
    c128 = jnp.concatenate([cos, cos, pad], axis=1)
    s128 = jnp.concatenate([-sin, sin, pad], axis=1)
    lead = jnp.zeros((pos.shape[0], MLA_KV_RANK), F32)
    return c128, s128, jnp.concatenate([lead, c128], axis=1), jnp.concatenate([lead, s128], axis=1)


def _bias_tiles(t5):
    i = jnp.arange(TQ, dtype=jnp.int32)
    rel = jnp.arange(2, dtype=jnp.int32)[:, None, None] * TQ + i[None, :, None] - i[None, None, :]
    far = t5[N_BUCKETS - 1]
    tiles = jnp.where((rel >= 0)[..., None], t5[_rel_bucket(rel)] - far, 0.0)
    return jnp.transpose(tiles, (0, 3, 1, 2)), far.reshape(-1, 1, 1)


def _run_trunk(x, mods, past, W, LW, t5_table):
    B, T, D = x.shape
    n = B * T
    depth = len(LW)
    n_past = 0 if past is None else past[1].shape[1] * PAGE_SIZE
    rope = _rope_tables(n_past + jnp.arange(T, dtype=jnp.int32))
    if past is None:
        bias_a = _bias_tiles(t5_table[:, :N_HEADS])
        bias_c = _bias_tiles(t5_table[:, N_HEADS:])
    xf = x.reshape(n, D)
    delta = gate = None
    per_layer = []
    for l in range(depth):
        lw = LW[l]
        sh1, sc1, g1, sh2, sc2, g2 = jnp.split(mods[l], 6, axis=-1)
        xf, h = _norm(xf, T, W['norm_mix'][l], delta, gate, sc1, sh1)
        z = _mm(h, lw['w_small'])
        logf, d_lat, d_rope, q_mla, kc = _prep(z, T, lw, rope)
        rows = (z[:, Z_KVA:Z_KVA + 2 * HEAD_DIM].reshape(B, T, 2, HEAD_DIM),
                z[:, Z_KIDX:Z_KIDX + IDX_DIM].reshape(B, T, IDX_DIM),
                z[:, Z_KVB:Z_KVB + 2 * HEAD_DIM].reshape(B, T, 2, HEAD_DIM),
                logf[:, :N_HEADS].reshape(B, T, N_HEADS),
                z[:, Z_KVC:Z_KVC + 2 * HEAD_DIM].reshape(B, T, 2, HEAD_DIM),
                d_lat.reshape(B, T, MLA_KV_RANK),
                d_rope[:, :MLA_ROPE].reshape(B, T, MLA_ROPE))
        if past is None:
            o = _prompt_mixers(z, logf, q_mla, kc, lw, bias_a, bias_c, B, T)
        else:
            assert T == 1
            o = _paged_mixers(z, rows, logf, q_mla, l, lw, t5_table, past[0], past[1], B)
        mixed = _gate_mix(h, lw['w_gates'], o, lw['w_branch'])
        mix = _mm(mixed, lw['w_out'])
        xf, h2 = _norm(xf, T, W['norm_ffn'][l], mix, g1, sc2, sh2)
        delta, gate = _peer(h2, lw), g2
        per_layer.append(rows)
    _, y = _norm(xf, T, W['norm_final'], delta, gate, out_dtype=F32)
    stacked = tuple(jnp.stack([r[i] for r in per_layer], axis=0) for i in range(7))
    return y.reshape(B, T, D), stacked


def kernel(x_prompt, x_sample, cache_a_kv, cache_a_idx, cache_b_kv, cache_b_logf, cache_c_kv,
           cache_d_latent, cache_d_rope, page_table, c_prompt, c_sample, t5_table, w_ada, b_ada,
           norm_mix, norm_ffn, w_in, b_forget, g_cq, g_ckv, w_uq, w_ukv, w_branch, w_out,
           w_peer_q, peer_k1, peer_k2, peer_u, peer_v, norm_final):
    W = {
        'norm_mix': norm_mix, 'norm_ffn': norm_ffn, 'w_in': w_in, 'b_forget': b_forget, 'g_cq': g_cq,
        'g_ckv': g_ckv, 'w_uq': w_uq, 'w_ukv': w_ukv, 'w_branch': w_branch, 'w_out': w_out,
        'w_peer_q': w_peer_q, 'peer_k1': peer_k1, 'peer_k2': peer_k2, 'peer_u': peer_u,
        'peer_v': peer_v, 'norm_final': norm_final,
    }
    caches = {
        'a_kv': cache_a_kv, 'a_idx': cache_a_idx, 'b_kv': cache_b_kv, 'b_logf': cache_b_logf,
        'c_kv': cache_c_kv, 'd_lat': cache_d_latent, 'd_rope': cache_d_rope,
    }
    depth, D = w_in.shape[0], w_in.shape[1]
    n_prompt = c_prompt.shape[0]
    LW = [_layer_weights(l, W, D) for l in range(depth)]
    c_all = jnp.concatenate([c_prompt, c_sample], axis=0)
    mods = [_mm(c_all, w_ada[l], bias=b_ada[l], silu_in=True) for l in range(depth)]
    y_prompt, rows_p = _run_trunk(x_prompt, [m[:n_prompt] for m in mods], None, W, LW, t5_table)
    y_sample, rows_s = _run_trunk(x_sample, [m[n_prompt:] for m in mods], (caches, page_table), W, LW, t5_table)
    return (y_prompt, y_sample) + tuple(rows_p) + tuple(rows_s)
```
